```python
import jax
import jax.numpy as jnp
from jax import lax
import numpy as np

D_MODEL = 1024
BATCH = 8
SEQ = 2048
DEPTH = 4
DEC_BATCH = 128
DEC_SEQ = 4
PAST_LEN = 8192
PAGE_SIZE = 128

N_A_LAYERS = DEPTH // 2
N_B_LAYERS = DEPTH - N_A_LAYERS
N_HEADS_A = D_MODEL // 64
NOPE_A = 64
ROPE_A = 32
V_DIM_A = 64
Q_LORA = D_MODEL // 4
KV_LORA = D_MODEL // 8
MLA_SCALE = (NOPE_A + ROPE_A) ** -0.5
WINDOWS = (128, 512, 2048)
DILATIONS = (1, 4, 16)
N_GROUPS = 3
HEADS_PER_GROUP = 8
HEAD_DIM_B = 64
N_HEADS_B = N_GROUPS * HEADS_PER_GROUP
ROT_DIM_B = HEAD_DIM_B // 4
B_SCALE = HEAD_DIM_B ** -0.5
BLOCK_B = 128
ROPE_THETA = 500000.0
D_FF = 2816
CONV_W = 3
ALPHA = (2.0 * DEPTH) ** 0.25
BETA = (8.0 * DEPTH) ** -0.25
Q_BLOCK = 128
LN_EPS = 1e-5
RMS_EPS = 1e-6
NEG_INF = -1e30

kernel_name = 'yoco_mla_dilated_convffn_step'


def window_lengths(length):
    return tuple(min(w, length) for w in WINDOWS)


def layer_norm(x, g, b):
    xf = x.astype(jnp.float32)
    mu = jnp.mean(xf, axis=-1, keepdims=True)
    var = jnp.mean(jnp.square(xf - mu), axis=-1, keepdims=True)
    return ((xf - mu) * lax.rsqrt(var + LN_EPS) * g + b).astype(x.dtype)


def rms_norm(x, g):
    xf = x.astype(jnp.float32)
    return (xf * lax.rsqrt(jnp.mean(jnp.square(xf), axis=-1, keepdims=True) + RMS_EPS) * g).astype(x.dtype)


def rope(x, pos, rot_dim):
    half = rot_dim // 2
    inv = ROPE_THETA ** (-jnp.arange(half, dtype=jnp.float32) * 2.0 / rot_dim)
    ang = pos.astype(jnp.float32)[:, None] * inv[None, :]
    cos = jnp.cos(ang)[:, None, :].astype(x.dtype)
    sin = jnp.sin(ang)[:, None, :].astype(x.dtype)
    x1 = x[..., :half]
    x2 = x[..., half:rot_dim]
    return jnp.concatenate([x1 * cos - x2 * sin, x2 * cos + x1 * sin, x[..., rot_dim:]], axis=-1)


def masked_softmax_lse(s, mask):
    s = jnp.where(mask, s, NEG_INF)
    m = jnp.max(s, axis=-1, keepdims=True)
    e = jnp.exp(s - m)
    den = jnp.sum(e, axis=-1, keepdims=True)
    return e / den, (m + jnp.log(den))[..., 0]


def mla_project(x, pos, w_in, g_q, g_kv, w_uq, w_uk):
    B, S, _ = x.shape
    a = x @ w_in
    c_q = rms_norm(a[..., :Q_LORA], g_q)
    c_kv = rms_norm(a[..., Q_LORA:Q_LORA + KV_LORA], g_kv)
    k_rope = rope(a[..., Q_LORA + KV_LORA:][:, :, None, :], pos, ROPE_A)[:, :, 0]
    q = (c_q @ w_uq).reshape(B, S, N_HEADS_A, NOPE_A + ROPE_A)
    q_rope = rope(q[..., NOPE_A:], pos, ROPE_A)
    q_lat = jnp.einsum('bshn,chn->bshc', q[..., :NOPE_A], w_uk)
    return q_lat, q_rope, c_kv, k_rope


def mla_attend(q_lat, q_rope, c_kv, k_rope, q_pos, k_pos):
    s = jnp.einsum('bqhc,bkc->bhqk', q_lat, c_kv) + jnp.einsum('bqhr,bkr->bhqk', q_rope, k_rope)
    s = jnp.where(k_pos[None, :] <= q_pos[:, None], s.astype(jnp.float32) * MLA_SCALE, NEG_INF)
    p = jax.nn.softmax(s, axis=-1).astype(c_kv.dtype)
    return jnp.einsum('bhqk,bkc->bqhc', p, c_kv)


def mla_output(out_lat, w_uv, w_o):
    B, S = out_lat.shape[:2]
    o = jnp.einsum('bshc,chv->bshv', out_lat, w_uv)
    return o.reshape(B, S, N_HEADS_A * V_DIM_A) @ w_o


def mla_prompt(x, pos, w_in, g_q, g_kv, w_uq, w_uk, w_uv, w_o):
    B, S, _ = x.shape
    q_lat, q_rope, c_kv, k_rope = mla_project(x, pos, w_in, g_q, g_kv, w_uq, w_uk)
    nb = S // Q_BLOCK

    def blocks(a):
        return a.reshape((B, nb, Q_BLOCK) + a.shape[2:]).swapaxes(0, 1)

    def attend_block(args):
        ql, qr, qp = args
        return mla_attend(ql, qr, c_kv, k_rope, qp, pos)

    out = lax.map(attend_block, (blocks(q_lat), blocks(q_rope), pos.reshape(nb, Q_BLOCK)))
    out = out.swapaxes(0, 1).reshape(B, S, N_HEADS_A, KV_LORA)
    return mla_output(out, w_uv, w_o), c_kv, k_rope


def gather_pages(cache, page_table):
    return cache[page_table].reshape(page_table.shape[0], -1, cache.shape[-1])


def mla_sample(x, pos, cache_lat, cache_rope, page_table, w_in, g_q, g_kv, w_uq, w_uk, w_uv, w_o):
    q_lat, q_rope, c_new, r_new = mla_project(x, pos, w_in, g_q, g_kv, w_uq, w_uk)
    c_all = jnp.concatenate([gather_pages(cache_lat, page_table), c_new], axis=1)
    r_all = jnp.concatenate([gather_pages(cache_rope, page_table), r_new], axis=1)
    k_pos = jnp.arange(c_all.shape[1], dtype=jnp.int32)
    out = mla_attend(q_lat, q_rope, c_all, r_all, pos, k_pos)
    return mla_output(out, w_uv, w_o), c_new, r_new


def dil_query(x, pos, w_q):
    B, S, _ = x.shape
    return rope((x @ w_q).reshape(B, S, N_HEADS_B, HEAD_DIM_B), pos, ROT_DIM_B)


def dil_keys_values(x, pos, w_kv):
    B, S, _ = x.shape
    kv = (x @ w_kv).reshape(B, S, 2, N_HEADS_B, HEAD_DIM_B)
    return rope(kv[:, :, 0], pos, ROT_DIM_B), kv[:, :, 1]


def dilated_prompt_group(q, k, v, d, span):
    B, S, H, D = q.shape
    P = S // d
    nb = -(-P // BLOCK_B)
    padn = nb * BLOCK_B - P

    def to_blocks(a):
        a = a.reshape(B, P, d, H, D).transpose(0, 2, 1, 3, 4)
        a = jnp.pad(a, ((0, 0), (0, 0), (0, padn), (0, 0), (0, 0)))
        return a.reshape(B, d, nb, BLOCK_B, H, D)

    def with_prev(a):
        prev = jnp.pad(a, ((0, 0), (0, 0), (1, 0), (0, 0), (0, 0), (0, 0)))[:, :, :nb]
        return jnp.concatenate([prev, a], axis=3)

    qb = to_blocks(q)
    kc = with_prev(to_blocks(k))
    vc = with_prev(to_blocks(v))
    s = jnp.einsum('brnqhd,brnkhd->brnhqk', qb, kc).astype(jnp.float32) * B_SCALE
    n = jnp.arange(nb)[:, None, None]
    a = jnp.arange(BLOCK_B)[None, :, None]
    c = jnp.arange(2 * BLOCK_B)[None, None, :]
    diff = a + BLOCK_B - c
    mask = (diff >= 0) & (diff <= span) & ((n - 1) * BLOCK_B + c >= 0)
    p, lse = masked_softmax_lse(s, mask[None, None, :, None])
    o = jnp.einsum('brnhqk,brnkhd->brnqhd', p.astype(v.dtype), vc)
    o = o.reshape(B, d, nb * BLOCK_B, H, D)[:, :, :P].transpose(0, 2, 1, 3, 4).reshape(B, S, H, D)
    lse = lse.transpose(0, 1, 2, 4, 3).reshape(B, d, nb * BLOCK_B, H)[:, :, :P]
    lse = lse.transpose(0, 2, 1, 3).reshape(B, S, H)
    return o, lse


def dilated_sample_group(q, k_all, v_all, buf_len, d, span):
    Bd, T, H, D = q.shape
    m = jnp.arange(span + 1)
    idx = buf_len + jnp.arange(T)[:, None] - m[None, :] * d
    valid = idx >= 0
    idx = jnp.maximum(idx, 0).reshape(-1)
    kg = jnp.take(k_all, idx, axis=1).reshape(Bd, T, span + 1, H, D)
    vg = jnp.take(v_all, idx, axis=1).reshape(Bd, T, span + 1, H, D)
    s = jnp.einsum('bthd,btmhd->bhtm', q, kg).astype(jnp.float32) * B_SCALE
    p, lse = masked_softmax_lse(s, valid[None, None])
    o = jnp.einsum('bhtm,btmhd->bthd', p.astype(vg.dtype), vg)
    return o, lse.transpose(0, 2, 1)


def merge_groups(outs, lses, w_o):
    o = jnp.stack(outs, 0)
    w = jax.nn.softmax(jnp.stack(lses, 0), axis=0).astype(o.dtype)
    y = jnp.einsum('gbsh,gbshd->bshd', w, o)
    B, S = y.shape[:2]
    return y.reshape(B, S, HEADS_PER_GROUP * HEAD_DIM_B) @ w_o


def dil_prompt(x, pos, k, v, w_q, w_o):
    q = dil_query(x, pos, w_q)
    outs, lses = [], []
    for g in range(N_GROUPS):
        hs = slice(g * HEADS_PER_GROUP, (g + 1) * HEADS_PER_GROUP)
        o, l = dilated_prompt_group(q[:, :, hs], k[:, :, hs], v[:, :, hs], DILATIONS[g], WINDOWS[g] // DILATIONS[g])
        outs.append(o)
        lses.append(l)
    return merge_groups(outs, lses, w_o)


def dil_sample(x, pos, k_alls, v_alls, lens, w_q, w_o):
    q = dil_query(x, pos, w_q)
    outs, lses = [], []
    for g in range(N_GROUPS):
        hs = slice(g * HEADS_PER_GROUP, (g + 1) * HEADS_PER_GROUP)
        o, l = dilated_sample_group(q[:, :, hs], k_alls[g], v_alls[g], lens[g], DILATIONS[g], WINDOWS[g] // DILATIONS[g])
        outs.append(o)
        lses.append(l)
    return merge_groups(outs, lses, w_o)


def window_contexts(buf_k, buf_v, k_new, v_new, past_len):
    lens = window_lengths(past_len)
    k_alls, v_alls = [], []
    off = 0
    for g in range(N_GROUPS):
        L = lens[g]
        hs = slice(g * HEADS_PER_GROUP, (g + 1) * HEADS_PER_GROUP)
        k_alls.append(jnp.concatenate([buf_k[:, off:off + L], k_new[:, :, hs]], axis=1))
        v_alls.append(jnp.concatenate([buf_v[:, off:off + L], v_new[:, :, hs]], axis=1))
        off += L
    return k_alls, v_alls, lens


def conv_ffn(x, conv_state, w_in, conv_w, conv_b, w_out):
    T = x.shape[1]
    u = x @ w_in
    g, val = u[..., :D_FF], u[..., D_FF:]
    gpad = jnp.concatenate([conv_state, g], axis=1)
    acc = conv_b
    for j in range(CONV_W):
        acc = acc + conv_w[j] * gpad[:, j:j + T]
    h = jax.nn.gelu(acc, approximate=False) * val
    return h @ w_out, gpad[:, gpad.shape[1] - (CONV_W - 1):]


def setup_inputs(seed: int = 0) -> dict:
    key = jax.random.key(seed)
    ks = jax.random.split(key, 32)
    f32 = jnp.float32

    def nrm(k, shape, scale=1.0):
        return jax.random.normal(k, shape, f32) * scale

    n_pages = PAST_LEN // PAGE_SIZE
    n_used = DEC_BATCH * n_pages
    n_pool = n_used + n_used // 4
    win_rows = sum(window_lengths(PAST_LEN))
    page_table = jax.random.permutation(ks[7], n_pool)[:n_used].reshape(DEC_BATCH, n_pages).astype(jnp.int32)
    return {
        'x_prompt': nrm(ks[0], (BATCH, SEQ, D_MODEL)),
        'x_sample': nrm(ks[1], (DEC_BATCH, DEC_SEQ, D_MODEL)),
        'cache_mla_latent': nrm(ks[2], (N_A_LAYERS, n_pool, PAGE_SIZE, KV_LORA)),
        'cache_mla_rope': nrm(ks[3], (N_A_LAYERS, n_pool, PAGE_SIZE, ROPE_A)),
        'cache_win_k': nrm(ks[4], (DEC_BATCH, win_rows, HEADS_PER_GROUP, HEAD_DIM_B)),
        'cache_win_v': nrm(ks[5], (DEC_BATCH, win_rows, HEADS_PER_GROUP, HEAD_DIM_B)),
        'state_conv': nrm(ks[6], (DEPTH, DEC_BATCH, CONV_W - 1, D_FF)),
        'page_table': page_table,
        'mla_w_in': nrm(ks[8], (N_A_LAYERS, D_MODEL, Q_LORA + KV_LORA + ROPE_A), D_MODEL ** -0.5),
        'mla_g_q': 1.0 + nrm(ks[9], (N_A_LAYERS, Q_LORA), 0.01),
        'mla_g_kv': 1.0 + nrm(ks[10], (N_A_LAYERS, KV_LORA), 0.01),
        'mla_w_uq': nrm(ks[11], (N_A_LAYERS, Q_LORA, N_HEADS_A * (NOPE_A + ROPE_A)), Q_LORA ** -0.5),
        'mla_w_uk': nrm(ks[12], (N_A_LAYERS, KV_LORA, N_HEADS_A, NOPE_A), KV_LORA ** -0.5),
        'mla_w_uv': nrm(ks[13], (N_A_LAYERS, KV_LORA, N_HEADS_A, V_DIM_A), KV_LORA ** -0.5),
        'mla_w_o': nrm(ks[14], (N_A_LAYERS, N_HEADS_A * V_DIM_A, D_MODEL), BETA * (N_HEADS_A * V_DIM_A) ** -0.5),
        'dil_w_kv': nrm(ks[15], (D_MODEL, 2 * N_HEADS_B * HEAD_DIM_B), D_MODEL ** -0.5),
        'dil_w_q': nrm(ks[16], (N_B_LAYERS, D_MODEL, N_HEADS_B * HEAD_DIM_B), D_MODEL ** -0.5),
        'dil_w_o': nrm(ks[17], (N_B_LAYERS, HEADS_PER_GROUP * HEAD_DIM_B, D_MODEL), BETA * (HEADS_PER_GROUP * HEAD_DIM_B) ** -0.5),
        'ffn_w_in': nrm(ks[18], (DEPTH, D_MODEL, 2 * D_FF), D_MODEL ** -0.5),
        'ffn_conv_w': nrm(ks[19], (DEPTH, CONV_W, D_FF), CONV_W ** -0.5),
        'ffn_conv_b': nrm(ks[20], (DEPTH, D_FF), 0.01),
        'ffn_w_out': nrm(ks[21], (DEPTH, D_FF, D_MODEL), BETA * D_FF ** -0.5),
        'ln_mix_g': 1.0 + nrm(ks[22], (DEPTH, D_MODEL), 0.01),
        'ln_mix_b': nrm(ks[23], (DEPTH, D_MODEL), 0.01),
        'ln_ffn_g': 1.0 + nrm(ks[24], (DEPTH, D_MODEL), 0.01),
        'ln_ffn_b': nrm(ks[25], (DEPTH, D_MODEL), 0.01),
    }


def reference(x_prompt, x_sample, cache_mla_latent, cache_mla_rope, cache_win_k, cache_win_v, state_conv, page_table,
              mla_w_in, mla_g_q, mla_g_kv, mla_w_uq, mla_w_uk, mla_w_uv, mla_w_o,
              dil_w_kv, dil_w_q, dil_w_o,
              ffn_w_in, ffn_conv_w, ffn_conv_b, ffn_w_out,
              ln_mix_g, ln_mix_b, ln_ffn_g, ln_ffn_b):
    S = x_prompt.shape[1]
    T = x_sample.shape[1]
    past_len = page_table.shape[1] * PAGE_SIZE
    pos_p = jnp.arange(S, dtype=jnp.int32)
    pos_s = past_len + jnp.arange(T, dtype=jnp.int32)
    xp, xs = x_prompt, x_sample
    lat_p, rope_p, lat_s, rope_s, conv_p, conv_s = [], [], [], [], [], []
    for layer in range(DEPTH):
        if layer < N_A_LAYERS:
            wa = (mla_w_in[layer], mla_g_q[layer], mla_g_kv[layer], mla_w_uq[layer], mla_w_uk[layer],
                  mla_w_uv[layer], mla_w_o[layer])
            mix_p, c_p, r_p = mla_prompt(xp, pos_p, *wa)
            mix_s, c_s, r_s = mla_sample(xs, pos_s, cache_mla_latent[layer], cache_mla_rope[layer], page_table, *wa)
            lat_p.append(c_p)
            rope_p.append(r_p)
            lat_s.append(c_s)
            rope_s.append(r_s)
        else:
            if layer == N_A_LAYERS:
                k_p, v_p = dil_keys_values(xp, pos_p, dil_w_kv)
                lens_p = window_lengths(S)
                new_win_k_prompt = jnp.concatenate(
                    [k_p[:, S - lens_p[g]:, g * HEADS_PER_GROUP:(g + 1) * HEADS_PER_GROUP] for g in range(N_GROUPS)], axis=1)
                new_win_v_prompt = jnp.concatenate(
                    [v_p[:, S - lens_p[g]:, g * HEADS_PER_GROUP:(g + 1) * HEADS_PER_GROUP] for g in range(N_GROUPS)], axis=1)
                k_new, v_new = dil_keys_values(xs, pos_s, dil_w_kv)
                k_alls, v_alls, lens_s = window_contexts(cache_win_k, cache_win_v, k_new, v_new, past_len)
                new_win_k_sample = jnp.concatenate([k_alls[g][:, T:] for g in range(N_GROUPS)], axis=1)
                new_win_v_sample = jnp.concatenate([v_alls[g][:, T:] for g in range(N_GROUPS)], axis=1)
            bi = layer - N_A_LAYERS
            mix_p = dil_prompt(xp, pos_p, k_p, v_p, dil_w_q[bi], dil_w_o[bi])
            mix_s = dil_sample(xs, pos_s, k_alls, v_alls, lens_s, dil_w_q[bi], dil_w_o[bi])
        xp = layer_norm(ALPHA * xp + mix_p, ln_mix_g[layer], ln_mix_b[layer])
        xs = layer_norm(ALPHA * xs + mix_s, ln_mix_g[layer], ln_mix_b[layer])
        wf = (ffn_w_in[layer], ffn_conv_w[layer], ffn_conv_b[layer], ffn_w_out[layer])
        f_p, st_p = conv_ffn(xp, jnp.zeros((xp.shape[0], CONV_W - 1, D_FF), xp.dtype), *wf)
        f_s, st_s = conv_ffn(xs, state_conv[layer], *wf)
        conv_p.append(st_p)
        conv_s.append(st_s)
        xp = layer_norm(ALPHA * xp + f_p, ln_ffn_g[layer], ln_ffn_b[layer])
        xs = layer_norm(ALPHA * xs + f_s, ln_ffn_g[layer], ln_ffn_b[layer])
    y_prompt = xp
    y_sample = xs
    new_mla_latent_prompt = jnp.stack(lat_p, 0)
    new_mla_rope_prompt = jnp.stack(rope_p, 0)
    new_conv_prompt = jnp.stack(conv_p, 0)
    new_mla_latent_sample = jnp.stack(lat_s, 0)
    new_mla_rope_sample = jnp.stack(rope_s, 0)
    new_conv_sample = jnp.stack(conv_s, 0)
    return (y_prompt, y_sample, new_mla_latent_prompt, new_mla_rope_prompt, new_win_k_prompt, new_win_v_prompt,
            new_conv_prompt, new_mla_latent_sample, new_mla_rope_sample, new_win_k_sample, new_win_v_sample,
            new_conv_sample)
```

```python
import functools

import numpy as np
import jax
import jax.numpy as jnp
from jax import lax
from jax.experimental import pallas as pl
from jax.experimental.pallas import tpu as pltpu

F32 = jnp.float32
BF16 = jnp.bfloat16

D_MODEL = 1024
DEPTH = 4
PAGE_SIZE = 128
N_A_LAYERS = DEPTH // 2
N_HEADS_A = 16
NOPE_A = 64
ROPE_A = 32
V_DIM_A = 64
Q_LORA = 256
KV_LORA = 128
MLA_SCALE = (NOPE_A + ROPE_A) ** -0.5
WINDOWS = (128, 512, 2048)
DILATIONS = (1, 4, 16)
N_GROUPS = 3
HEADS_PER_GROUP = 8
HEAD_DIM_B = 64
N_HEADS_B = N_GROUPS * HEADS_PER_GROUP
ROT_DIM_B = HEAD_DIM_B // 4
B_SCALE = HEAD_DIM_B ** -0.5
BLOCK_B = 128
ROPE_THETA = 500000.0
D_FF = 2816
CONV_W = 3
ALPHA = (2.0 * DEPTH) ** 0.25
LN_EPS = 1e-5
RMS_EPS = 1e-6
NEG_INF = -1e30
SQRT_HALF = 0.7071067811865476

LANES = 128
SUBLANES = 8
VMEM_LIMIT_BYTES = 48 * 1024 * 1024

GROUP_W = HEADS_PER_GROUP * HEAD_DIM_B
QB_W = N_HEADS_B * HEAD_DIM_B
MLA_QK_W = 2 * LANES
FFN_CHUNK = 256


def _params(*sem):
    return pltpu.CompilerParams(dimension_semantics=sem, vmem_limit_bytes=VMEM_LIMIT_BYTES)


def _const_spec(shape):
    nd = len(shape)
    return pl.BlockSpec(shape, lambda *_: (0,) * nd)


def _dot(a, b):
    return jnp.dot(a, b, preferred_element_type=F32)


def _dot_nt(a, b):
    return lax.dot_general(a, b, (((1,), (1,)), ((), ())), preferred_element_type=F32)


def _layer_norm(z, g, b):
    mu = jnp.mean(z, axis=-1, keepdims=True)
    zc = z - mu
    var = jnp.mean(zc * zc, axis=-1, keepdims=True)
    return zc * lax.rsqrt(var + LN_EPS) * g + b


def _rms_norm(a, g):
    return a * lax.rsqrt(jnp.mean(a * a, axis=-1, keepdims=True) + RMS_EPS) * g


def _rope_lanes(v, cos, sin_a, sin_b, half):
    return (v * cos + pltpu.roll(v, LANES - half, 1) * sin_a + pltpu.roll(v, half, 1) * sin_b)


def _rope_tables(pos, rot_dim, period):
    half = rot_dim // 2
    inv = ROPE_THETA ** (-jnp.arange(half, dtype=F32) * 2.0 / rot_dim)
    ang = pos.astype(F32)[:, None] * inv[None, :]
    cos, sin = jnp.cos(ang), jnp.sin(ang)
    t = pos.shape[0]
    ones = jnp.ones((t, period - rot_dim), F32)
    zr = lambda n: jnp.zeros((t, n), F32)
    c = jnp.concatenate([cos, cos, ones], axis=1)
    sa = jnp.concatenate([-sin, zr(period - half)], axis=1)
    sb = jnp.concatenate([zr(half), sin, zr(period - rot_dim)], axis=1)
    rep = LANES // period
    return tuple(jnp.tile(a, (1, rep)) for a in (c, sa, sb))


def _mla_proj_kernel(x_ref, win_ref, gq_ref, gkv_ref, cos_ref, sa_ref, sb_ref, wn_ref, wr_ref, wuk_ref,
                     ckv_ref, kr_ref, kcat_ref, q_ref, *, tm):
    xb = x_ref[...].astype(BF16)
    a = _dot(xb, win_ref[...])
    cq = _rms_norm(a[:, :Q_LORA], gq_ref[...])
    ckv = _rms_norm(a[:, Q_LORA:Q_LORA + KV_LORA], gkv_ref[...])
    cos, sa, sb = cos_ref[...], sa_ref[...], sb_ref[...]
    kr = _rope_lanes(a[:, Q_LORA + KV_LORA:], cos, sa, sb, ROPE_A // 2)
    ckv_ref[...] = ckv
    kr_ref[...] = kr
    kcat_ref[:, :LANES] = ckv.astype(BF16)
    kcat_ref[:, LANES:] = kr.astype(BF16)
    cqb = cq.astype(BF16)
    qn = _dot(cqb, wn_ref[...])
    qr = _dot(cqb, wr_ref[...])
    for p in range(N_HEADS_A // 2):
        ql2 = _dot(qn[:, p * LANES:(p + 1) * LANES].astype(BF16), wuk_ref[p])
        for hh in range(2):
            h = 2 * p + hh
            ql = ql2[:, hh * LANES:(hh + 1) * LANES].astype(BF16)
            rr = _rope_lanes(qr[:, h * LANES:(h + 1) * LANES], cos, sa, sb, ROPE_A // 2).astype(BF16)
            for j in range(tm // LANES):
                q_ref[j, h, :, :LANES] = ql[j * LANES:(j + 1) * LANES]
                q_ref[j, h, :, LANES:] = rr[j * LANES:(j + 1) * LANES]


def _mla_proj(x, w, tables, tm):
    m = x.shape[0]
    cos, sa, sb = tables
    nper = cos.shape[0] // tm
    tab_spec = pl.BlockSpec((tm, LANES), lambda i: (i % nper, 0))
    row = lambda n: pl.BlockSpec((tm, n), lambda i: (i, 0))
    return pl.pallas_call(
        functools.partial(_mla_proj_kernel, tm=tm),
        grid=(m // tm,),
        in_specs=[row(D_MODEL), _const_spec(w["w_in"].shape), _const_spec((1, Q_LORA)), _const_spec((1, KV_LORA)),
                  tab_spec, tab_spec, tab_spec, _const_spec(w["w_nope"].shape), _const_spec(w["w_rope"].shape),
                  _const_spec(w["w_uk"].shape)],
        out_specs=[row(KV_LORA), row(LANES), row(MLA_QK_W),
                   pl.BlockSpec((tm // LANES, N_HEADS_A, LANES, MLA_QK_W), lambda i: (i, 0, 0, 0))],
        out_shape=[jax.ShapeDtypeStruct((m, KV_LORA), F32), jax.ShapeDtypeStruct((m, LANES), F32),
                   jax.ShapeDtypeStruct((m, MLA_QK_W), BF16),
                   jax.ShapeDtypeStruct((m // LANES, N_HEADS_A, LANES, MLA_QK_W), BF16)],
        compiler_params=_params("parallel"),
        name="mla_proj",
    )(x, w["w_in"], w["g_q"], w["g_kv"], cos, sa, sb, w["w_nope"], w["w_rope"], w["w_uk"])


MLA_TQ = 128
MLA_TK = 256


def _mla_flash_kernel(qi_ref, kj_ref, q_ref, k_ref, o_ref, m_ref, l_ref, acc_ref):
    t = pl.program_id(1)
    qi = qi_ref[t]
    kj = kj_ref[t]
    rows = N_HEADS_A * MLA_TQ

    @pl.when(kj == 0)
    def _():
        m_ref[...] = jnp.full(m_ref.shape, NEG_INF, F32)
        l_ref[...] = jnp.zeros(l_ref.shape, F32)
        acc_ref[...] = jnp.zeros(acc_ref.shape, F32)

    q = q_ref[0].reshape(rows, MLA_QK_W)
    k = k_ref[...]
    s = _dot_nt(q, k) * MLA_SCALE
    q_pos = qi * MLA_TQ + (lax.broadcasted_iota(jnp.int32, s.shape, 0) & (MLA_TQ - 1))
    k_pos = kj * MLA_TK + lax.broadcasted_iota(jnp.int32, s.shape, 1)
    s = jnp.where(k_pos <= q_pos, s, NEG_INF)
    m_prev = m_ref[...]
    m_new = jnp.maximum(m_prev, jnp.max(s, axis=-1, keepdims=True))
    alpha = jnp.exp(m_prev - m_new)
    p = jnp.exp(s - m_new)
    l_ref[...] = alpha * l_ref[...] + jnp.sum(p, axis=-1, keepdims=True)
    acc_ref[...] = alpha * acc_ref[...] + _dot(p.astype(BF16), k[:, :KV_LORA])
    m_ref[...] = m_new

    @pl.when(kj == (qi * MLA_TQ) // MLA_TK)
    def _():
        o = acc_ref[...] / l_ref[...]
        for h in range(N_HEADS_A):
            o_ref[:, h * KV_LORA:(h + 1) * KV_LORA] = o[h * MLA_TQ:(h + 1) * MLA_TQ].astype(BF16)


def _mla_flash(q, kcat, batch, seq):
    nq = seq // MLA_TQ
    nk = seq // MLA_TK
    pairs = [(i, j) for i in range(nq) for j in range((i * MLA_TQ) // MLA_TK + 1)]
    qi = jnp.asarray(np.array([p[0] for p in pairs], np.int32))
    kj = jnp.asarray(np.array([p[1] for p in pairs], np.int32))
    rows = N_HEADS_A * MLA_TQ
    return pl.pallas_call(
        _mla_flash_kernel,
        grid_spec=pltpu.PrefetchScalarGridSpec(
            num_scalar_prefetch=2,
            grid=(batch, len(pairs)),
            in_specs=[pl.BlockSpec((1, N_HEADS_A, MLA_TQ, MLA_QK_W), lambda b, t, qi, kj: (b * nq + qi[t], 0, 0, 0)),
                      pl.BlockSpec((MLA_TK, MLA_QK_W), lambda b, t, qi, kj: (b * nk + kj[t], 0))],
            out_specs=pl.BlockSpec((MLA_TQ, N_HEADS_A * KV_LORA), lambda b, t, qi, kj: (b * nq + qi[t], 0)),
            scratch_shapes=[pltpu.VMEM((rows, 1), F32), pltpu.VMEM((rows, 1), F32), pltpu.VMEM((rows, KV_LORA), F32)],
        ),
        out_shape=jax.ShapeDtypeStruct((batch * seq, N_HEADS_A * KV_LORA), BF16),
        compiler_params=_params("parallel", "arbitrary"),
        name="mla_flash",
    )(qi, kj, q, kcat)


def _mla_sample_kernel(pt_ref, q_ref, cn_ref, rn_ref, lat_hbm, rope_hbm, o_ref, lat_buf, rope_buf, sem,
                       *, n_pages, n_new):
    b = pl.program_id(0)
    nb = pl.num_programs(0)
    slot = b % 2

    def lat_copy(bb, sl, j):
        return pltpu.make_async_copy(lat_hbm.at[pt_ref[bb, j]], lat_buf.at[sl, j], sem.at[sl, 0])

    def rope_copy(bb, sl, j):
        return pltpu.make_async_copy(rope_hbm.at[pt_ref[bb, j]], rope_buf.at[sl, j], sem.at[sl, 1])

    def start_all(bb, sl):
        for j in range(n_pages):
            lat_copy(bb, sl, j).start()
            rope_copy(bb, sl, j).start()

    @pl.when(b == 0)
    def _():
        start_all(0, 0)

    @pl.when(b + 1 < nb)
    def _():
        start_all(b + 1, 1 - slot)

    for j in range(n_pages):
        lat_copy(b, slot, j).wait()
        rope_copy(b, slot, j).wait()

    n_keys = n_pages * PAGE_SIZE
    lat = lat_buf[slot].reshape(n_keys, KV_LORA).astype(BF16)
    rope = rope_buf[slot].reshape(n_keys, ROPE_A).astype(BF16)
    q = q_ref[0]
    q_lat = q[:, :KV_LORA]
    cn = cn_ref[0].astype(BF16)
    rn = rn_ref[0].astype(BF16)
    s = (_dot_nt(q_lat, lat) + _dot_nt(q[:, KV_LORA:KV_LORA + ROPE_A], rope)) * MLA_SCALE
    s_new = (_dot_nt(q_lat, cn) + _dot_nt(q[:, KV_LORA:], rn)) * MLA_SCALE
    t_q = lax.broadcasted_iota(jnp.int32, s_new.shape, 0) // N_HEADS_A
    j_k = lax.broadcasted_iota(jnp.int32, s_new.shape, 1)
    s_new = jnp.where(j_k <= t_q, s_new, NEG_INF)
    m = jnp.maximum(jnp.max(s, axis=-1, keepdims=True), jnp.max(s_new, axis=-1, keepdims=True))
    e = jnp.exp(s - m)
    e_new = jnp.exp(s_new - m)
    den = jnp.sum(e, axis=-1, keepdims=True) + jnp.sum(e_new, axis=-1, keepdims=True)
    out = _dot((e / den).astype(BF16), lat) + _dot((e_new / den).astype(BF16), cn)
    o_ref[0] = out.astype(BF16)


def _mla_sample_attn(page_table, q, c_new, r_new, cache_lat, cache_rope):
    nb, n_pages = page_table.shape
    rows = q.shape[1]
    n_new = rows // N_HEADS_A
    return pl.pallas_call(
        functools.partial(_mla_sample_kernel, n_pages=n_pages, n_new=n_new),
        grid_spec=pltpu.PrefetchScalarGridSpec(
            num_scalar_prefetch=1,
            grid=(nb,),
            in_specs=[pl.BlockSpec((1, rows, MLA_QK_W), lambda b, pt: (b, 0, 0)),
                      pl.BlockSpec((1, SUBLANES, KV_LORA), lambda b, pt: (b, 0, 0)),
                      pl.BlockSpec((1, SUBLANES, LANES), lambda b, pt: (b, 0, 0)),
                      pl.BlockSpec(memory_space=pl.ANY),
                      pl.BlockSpec(memory_space=pl.ANY)],
            out_specs=pl.BlockSpec((1, rows, KV_LORA), lambda b, pt: (b, 0, 0)),
            scratch_shapes=[pltpu.VMEM((2, n_pages, PAGE_SIZE, KV_LORA), F32),
                            pltpu.VMEM((2, n_pages, PAGE_SIZE, ROPE_A), F32),
                            pltpu.SemaphoreType.DMA((2, 2))],
        ),
        out_shape=jax.ShapeDtypeStruct((nb, rows, KV_LORA), BF16),
        compiler_params=_params("arbitrary"),
        name="mla_sample_attn",
    )(page_table, q, c_new, r_new, cache_lat, cache_rope)


def _mla_out_kernel(ol_ref, wuv_ref, wo_ref, x_ref, g_ref, b_ref, y_ref):
    parts = [_dot(ol_ref[:, p * 2 * KV_LORA:(p + 1) * 2 * KV_LORA], wuv_ref[p]) for p in range(N_HEADS_A // 2)]
    o = jnp.concatenate(parts, axis=1).astype(BF16)
    mix = _dot(o, wo_ref[...])
    y_ref[...] = _layer_norm(ALPHA * x_ref[...] + mix, g_ref[...], b_ref[...])


def _mla_out(ol, w, x, g, b, tm):
    m = x.shape[0]
    row = lambda n: pl.BlockSpec((tm, n), lambda i: (i, 0))
    return pl.pallas_call(
        _mla_out_kernel,
        grid=(m // tm,),
        in_specs=[row(N_HEADS_A * KV_LORA), _const_spec(w["w_uv"].shape), _const_spec(w["w_o"].shape),
                  row(D_MODEL), _const_spec((1, D_MODEL)), _const_spec((1, D_MODEL))],
        out_specs=row(D_MODEL),
        out_shape=jax.ShapeDtypeStruct((m, D_MODEL), F32),
        compiler_params=_params("parallel"),
        name="mla_out",
    )(ol, w["w_uv"], w["w_o"], x, g, b)


def _proj_ln_kernel(a_ref, wo_ref, x_ref, g_ref, b_ref, y_ref):
    mix = _dot(a_ref[...].astype(BF16), wo_ref[...])
    y_ref[...] = _layer_norm(ALPHA * x_ref[...] + mix, g_ref[...], b_ref[...])


def _proj_ln(a, wo, x, g, b, tm):
    m = x.shape[0]
    row = lambda n: pl.BlockSpec((tm, n), lambda i: (i, 0))
    return pl.pallas_call(
        _proj_ln_kernel,
        grid=(m // tm,),
        in_specs=[row(a.shape[1]), _const_spec(wo.shape), row(D_MODEL), _const_spec((1, D_MODEL)),
                  _const_spec((1, D_MODEL))],
        out_specs=row(D_MODEL),
        out_shape=jax.ShapeDtypeStruct((m, D_MODEL), F32),
        compiler_params=_params("parallel"),
        name="proj_ln",
    )(a, wo, x, g, b)


def _dil_merge_kernel(o0, o1, o2, l0, l1, l2, wo_ref, x_ref, g_ref, b_ref, y_ref):
    la, lb, lc = l0[...], l1[...], l2[...]
    m = jnp.maximum(jnp.maximum(la, lb), lc)
    ea, eb, ec = jnp.exp(la - m), jnp.exp(lb - m), jnp.exp(lc - m)
    den = ea + eb + ec
    y = (ea / den) * o0[...] + (eb / den) * o1[...] + (ec / den) * o2[...]
    mix = _dot(y.astype(BF16), wo_ref[...])
    y_ref[...] = _layer_norm(ALPHA * x_ref[...] + mix, g_ref[...], b_ref[...])


def _dil_merge(outs, lses, wo, x, g, b, tm):
    m = x.shape[0]
    row = lambda n: pl.BlockSpec((tm, n), lambda i: (i, 0))
    return pl.pallas_call(
        _dil_merge_kernel,
        grid=(m // tm,),
        in_specs=[row(GROUP_W)] * 6 + [_const_spec(wo.shape), row(D_MODEL), _const_spec((1, D_MODEL)),
                                       _const_spec((1, D_MODEL))],
        out_specs=row(D_MODEL),
        out_shape=jax.ShapeDtypeStruct((m, D_MODEL), F32),
        compiler_params=_params("parallel"),
        name="dil_merge",
    )(*outs, *lses, wo, x, g, b)


def _gelu_gate(acc, val):
    return 0.5 * acc * (1.0 + lax.erf(acc * SQRT_HALF)) * val


def _ffn_prompt_kernel(x_ref, xh_ref, win_ref, cw_ref, cb_ref, wout_ref, g_ref, b_ref, y_ref, st_ref,
                       *, tm, tiles_per_seq):
    i = pl.program_id(0)
    x = x_ref[...]
    xb = x.astype(BF16)
    xhb = xh_ref[...].astype(BF16)
    keep = ((i % tiles_per_seq) != 0).astype(F32)
    row = lax.broadcasted_iota(jnp.int32, (tm, FFN_CHUNK), 0)
    acc = jnp.zeros((tm, D_MODEL), F32)
    for c in range(D_FF // FFN_CHUNK):
        lo = c * FFN_CHUNK
        wg = win_ref[:, lo:lo + FFN_CHUNK]
        gate = _dot(xb, wg)
        val = _dot(xb, win_ref[:, D_FF + lo:D_FF + lo + FFN_CHUNK])
        gh = _dot(xhb, wg) * keep
        g1 = jnp.where(row == 0, gh[SUBLANES - 1:SUBLANES], pltpu.roll(gate, 1, 0))
        g2 = jnp.where(row == 0, gh[SUBLANES - 2:SUBLANES - 1],
                       jnp.where(row == 1, gh[SUBLANES - 1:SUBLANES], pltpu.roll(gate, 2, 0)))
        a = cb_ref[:, lo:lo + FFN_CHUNK] + cw_ref[0:1, lo:lo + FFN_CHUNK] * g2
        a = a + cw_ref[1:2, lo:lo + FFN_CHUNK] * g1
        a = a + cw_ref[2:3, lo:lo + FFN_CHUNK] * gate
        h = _gelu_gate(a, val).astype(BF16)
        acc = acc + _dot(h, wout_ref[lo:lo + FFN_CHUNK, :])
        st_ref[0, :, lo:lo + FFN_CHUNK] = gate[tm - SUBLANES:]
    y_ref[...] = _layer_norm(ALPHA * x + acc, g_ref[...], b_ref[...])


def _ffn_prompt(x, w, g, b, seq, tm):
    m = x.shape[0]
    tiles_per_seq = seq // tm
    hb = tm // SUBLANES
    row = lambda n: pl.BlockSpec((tm, n), lambda i: (i, 0))
    y, st = pl.pallas_call(
        functools.partial(_ffn_prompt_kernel, tm=tm, tiles_per_seq=tiles_per_seq),
        grid=(m // tm,),
        in_specs=[row(D_MODEL),
                  pl.BlockSpec((SUBLANES, D_MODEL), lambda i: (jnp.maximum(i * hb - 1, 0), 0)),
                  _const_spec(w["w_in"].shape), _const_spec((CONV_W, D_FF)), _const_spec((1, D_FF)),
                  _const_spec(w["w_out"].shape), _const_spec((1, D_MODEL)), _const_spec((1, D_MODEL))],
        out_specs=[row(D_MODEL), pl.BlockSpec((1, SUBLANES, D_FF), lambda i: (i // tiles_per_seq, 0, 0))],
        out_shape=[jax.ShapeDtypeStruct((m, D_MODEL), F32),
                   jax.ShapeDtypeStruct((m // seq, SUBLANES, D_FF), F32)],
        compiler_params=_params("arbitrary"),
        name="ffn_prompt",
    )(x, x, w["w_in"], w["conv_w"], w["conv_b"], w["w_out"], g, b)
    return y, st[:, SUBLANES - (CONV_W - 1):]


def _ffn_sample_kernel(x_ref, s_ref, win_ref, cw_ref, cb_ref, wout_ref, g_ref, b_ref, y_ref, st_ref, *, nb, nt):
    x = x_ref[...]
    xb = x.astype(BF16)
    acc = jnp.zeros((nt * nb, D_MODEL), F32)
    for c in range(D_FF // FFN_CHUNK):
        lo = c * FFN_CHUNK
        gate = _dot(xb, win_ref[:, lo:lo + FFN_CHUNK])
        val = _dot(xb, win_ref[:, D_FF + lo:D_FF + lo + FFN_CHUNK])
        s0 = s_ref[0, :, lo:lo + FFN_CHUNK]
        s1 = s_ref[1, :, lo:lo + FFN_CHUNK]
        g1 = jnp.concatenate([s1, gate[:(nt - 1) * nb]], axis=0)
        g2 = jnp.concatenate([s0, s1, gate[:(nt - 2) * nb]], axis=0)
        a = cb_ref[:, lo:lo + FFN_CHUNK] + cw_ref[0:1, lo:lo + FFN_CHUNK] * g2
        a = a + cw_ref[1:2, lo:lo + FFN_CHUNK] * g1
        a = a + cw_ref[2:3, lo:lo + FFN_CHUNK] * gate
        h = _gelu_gate(a, val).astype(BF16)
        acc = acc + _dot(h, wout_ref[lo:lo + FFN_CHUNK, :])
        st_ref[0, :, lo:lo + FFN_CHUNK] = gate[(nt - 2) * nb:(nt - 1) * nb]
        st_ref[1, :, lo:lo + FFN_CHUNK] = gate[(nt - 1) * nb:]
    y_ref[...] = _layer_norm(ALPHA * x + acc, g_ref[...], b_ref[...])


def _ffn_sample(x, state, w, g, b, nb, nt):
    m = x.shape[0]
    return pl.pallas_call(
        functools.partial(_ffn_sample_kernel, nb=nb, nt=nt),
        out_shape=[jax.ShapeDtypeStruct((m, D_MODEL), F32), jax.ShapeDtypeStruct((CONV_W - 1, nb, D_FF), F32)],
        compiler_params=pltpu.CompilerParams(vmem_limit_bytes=VMEM_LIMIT_BYTES),
        name="ffn_sample",
    )(x, state, w["w_in"], w["conv_w"], w["conv_b"], w["w_out"], g, b)


PROJ_CHUNK = 512


def _proj_rope_kernel(x_ref, w_ref, cos_ref, sa_ref, sb_ref, o_ref, *, n, rope_cols):
    xb = x_ref[...].astype(BF16)
    cos, sa, sb = cos_ref[...], sa_ref[...], sb_ref[...]
    for c in range(n // PROJ_CHUNK):
        lo = c * PROJ_CHUNK
        y = _dot(xb, w_ref[:, lo:lo + PROJ_CHUNK])
        if lo < rope_cols:
            y = jnp.concatenate(
                [_rope_lanes(y[:, k * LANES:(k + 1) * LANES], cos, sa, sb, ROT_DIM_B // 2)
                 for k in range(PROJ_CHUNK // LANES)], axis=1)
        o_ref[:, lo:lo + PROJ_CHUNK] = y.astype(o_ref.dtype)


def _proj_rope(x, w, tables, rope_cols, out_dtype, tm):
    m = x.shape[0]
    n = w.shape[1]
    cos, sa, sb = tables
    nper = cos.shape[0] // tm
    tab_spec = pl.BlockSpec((tm, LANES), lambda i: (i % nper, 0))
    return pl.pallas_call(
        functools.partial(_proj_rope_kernel, n=n, rope_cols=rope_cols),
        grid=(m // tm,),
        in_specs=[pl.BlockSpec((tm, D_MODEL), lambda i: (i, 0)), _const_spec(w.shape), tab_spec, tab_spec, tab_spec],
        out_specs=pl.BlockSpec((tm, n), lambda i: (i, 0)),
        out_shape=jax.ShapeDtypeStruct((m, n), out_dtype),
        compiler_params=_params("parallel"),
        name="proj_rope",
    )(x, w, cos, sa, sb)


def _dil_attn_kernel(*refs, has_prev, span):
    if has_prev:
        q_ref, kp_ref, kc_ref, vp_ref, vc_ref, o_ref, l_ref = refs
        k = jnp.concatenate([kp_ref[0], kc_ref[0]], axis=0).astype(BF16)
        v = jnp.concatenate([vp_ref[0], vc_ref[0]], axis=0).astype(BF16)
    else:
        q_ref, kc_ref, vc_ref, o_ref, l_ref = refs
        k = kc_ref[0].astype(BF16)
        v = vc_ref[0].astype(BF16)
    n = pl.program_id(2)
    q = q_ref[0]
    nk = k.shape[0]
    a = lax.broadcasted_iota(jnp.int32, (BLOCK_B, nk), 0)
    c = lax.broadcasted_iota(jnp.int32, (BLOCK_B, nk), 1)
    if has_prev:
        diff = a + BLOCK_B - c
        mask = (diff >= 0) & (diff <= span) & ((n - 1) * BLOCK_B + c >= 0)
    else:
        diff = a - c
        mask = (diff >= 0) & (diff <= span)
    for h in range(HEADS_PER_GROUP):
        hs = slice(h * HEAD_DIM_B, (h + 1) * HEAD_DIM_B)
        s = _dot_nt(q[:, hs], k[:, hs]) * B_SCALE
        s = jnp.where(mask, s, NEG_INF)
        m = jnp.max(s, axis=-1, keepdims=True)
        e = jnp.exp(s - m)
        den = jnp.sum(e, axis=-1, keepdims=True)
        o_ref[0, :, hs] = _dot((e / den).astype(BF16), v[:, hs])
        l_ref[0, :, hs] = jnp.broadcast_to(m + jnp.log(den), (BLOCK_B, HEAD_DIM_B))


def _dil_attn_group(q, kv, batch, seq, g):
    d = DILATIONS[g]
    span = WINDOWS[g] // d
    p_len = seq // d
    nb = p_len // BLOCK_B
    has_prev = nb > 1
    qv = q.reshape(batch, p_len, d * QB_W)
    kvv = kv.reshape(batch, p_len, d * 2 * QB_W)
    nq, nkv = QB_W // GROUP_W, 2 * QB_W // GROUP_W
    blk = (1, BLOCK_B, GROUP_W)
    q_spec = pl.BlockSpec(blk, lambda b, r, n: (b, n, r * nq + g))
    kc = pl.BlockSpec(blk, lambda b, r, n: (b, n, r * nkv + g))
    vc = pl.BlockSpec(blk, lambda b, r, n: (b, n, r * nkv + nq + g))
    kp = pl.BlockSpec(blk, lambda b, r, n: (b, jnp.maximum(n - 1, 0), r * nkv + g))
    vp = pl.BlockSpec(blk, lambda b, r, n: (b, jnp.maximum(n - 1, 0), r * nkv + nq + g))
    in_specs = [q_spec, kp, kc, vp, vc] if has_prev else [q_spec, kc, vc]
    args = (qv, kvv, kvv, kvv, kvv) if has_prev else (qv, kvv, kvv)
    o_spec = pl.BlockSpec(blk, lambda b, r, n: (b, n, r))
    shape = jax.ShapeDtypeStruct((batch, p_len, d * GROUP_W), F32)
    o, l = pl.pallas_call(
        functools.partial(_dil_attn_kernel, has_prev=has_prev, span=span),
        grid=(batch, d, nb),
        in_specs=in_specs,
        out_specs=[o_spec, o_spec],
        out_shape=[shape, shape],
        compiler_params=_params("parallel", "parallel", "arbitrary"),
        name=f"dil_attn_g{g}",
    )(*args)
    return o.reshape(batch * seq, GROUP_W), l.reshape(batch * seq, GROUP_W)


def _dil_sample_layout(past_len, n_new):
    lens = tuple(min(w, past_len) for w in WINDOWS)
    offs = tuple(int(sum(lens[:g])) for g in range(N_GROUPS))
    return lens, offs


def _dil_sample_bias(seg_pos, seg_len, d, span, n_new):
    n = len(seg_pos)
    t = np.arange(n_new)[:, None, None, None]
    hq = np.arange(HEADS_PER_GROUP)[None, :, None, None]
    j = np.asarray(seg_pos)[None, None, :, None]
    hk = np.arange(HEADS_PER_GROUP)[None, None, None, :]
    rel = seg_len + t - j
    ok = (rel >= 0) & (rel % d == 0) & (rel // d <= span) & (hq == hk)
    bias = np.where(ok, 0.0, NEG_INF).astype(np.float32)
    return bias.reshape(n_new * HEADS_PER_GROUP, n * HEADS_PER_GROUP)


def _dil_sample_kernel(q_ref, kn_ref, vn_ref, b0_ref, b1_ref, b2_ref, bn_ref, k_hbm, v_hbm, o_ref,
                       ka_buf, kb_buf, va_buf, vb_buf, sem, *, head_rows, n_phase, phase_rows, lens, pos2):
    b = pl.program_id(0)
    nb = pl.num_programs(0)
    slot = b % 2
    a_blocks = ka_buf.shape[1]
    lo2 = a_blocks

    def copies(bb, sl):
        cps = [pltpu.make_async_copy(k_hbm.at[bb, pl.ds(0, a_blocks)], ka_buf.at[sl], sem.at[sl, 0]),
               pltpu.make_async_copy(v_hbm.at[bb, pl.ds(0, a_blocks)], va_buf.at[sl], sem.at[sl, 1])]
        for t in range(n_phase):
            cps.append(pltpu.make_async_copy(k_hbm.at[bb, pl.ds(lo2, phase_rows), t], kb_buf.at[sl, t],
                                             sem.at[sl, 2]))
            cps.append(pltpu.make_async_copy(v_hbm.at[bb, pl.ds(lo2, phase_rows), t], vb_buf.at[sl, t],
                                             sem.at[sl, 3]))
        return cps

    @pl.when(b == 0)
    def _():
        for cp in copies(0, 0):
            cp.start()

    @pl.when(b + 1 < nb)
    def _():
        for cp in copies(b + 1, 1 - slot):
            cp.start()

    for cp in copies(b, slot):
        cp.wait()

    hp = HEADS_PER_GROUP
    ka = ka_buf[slot].reshape(head_rows * hp, HEAD_DIM_B)
    va = va_buf[slot].reshape(head_rows * hp, HEAD_DIM_B)
    kb = kb_buf[slot].reshape(n_phase * phase_rows * hp, HEAD_DIM_B)
    vb = vb_buf[slot].reshape(n_phase * phase_rows * hp, HEAD_DIM_B)
    segs = [(ka[:lens[0] * hp], va[:lens[0] * hp], b0_ref),
            (ka[lens[0] * hp:], va[lens[0] * hp:], b1_ref),
            (kb, vb, b2_ref)]
    outs, lses = [], []
    for g, (kseg, vseg, bias_ref) in enumerate(segs):
        q = q_ref[0, g]
        kn = kn_ref[0, g].astype(BF16)
        vn = vn_ref[0, g].astype(BF16)
        kseg = kseg.astype(BF16)
        vseg = vseg.astype(BF16)
        s = _dot_nt(q, kseg) * B_SCALE + bias_ref[...]
        s_new = _dot_nt(q, kn) * B_SCALE + bn_ref[g]
        m = jnp.maximum(jnp.max(s, axis=-1, keepdims=True), jnp.max(s_new, axis=-1, keepdims=True))
        e = jnp.exp(s - m)
        e_new = jnp.exp(s_new - m)
        den = jnp.sum(e, axis=-1, keepdims=True) + jnp.sum(e_new, axis=-1, keepdims=True)
        outs.append(_dot((e / den).astype(BF16), vseg) + _dot((e_new / den).astype(BF16), vn))
        lses.append(m + jnp.log(den))
    m = jnp.maximum(jnp.maximum(lses[0], lses[1]), lses[2])
    ws = [jnp.exp(l - m) for l in lses]
    den = ws[0] + ws[1] + ws[2]
    o_ref[0] = (ws[0] / den) * outs[0] + (ws[1] / den) * outs[1] + (ws[2] / den) * outs[2]


def _dil_sample_attn(q, k_new, v_new, cache_k, cache_v, past_len):
    nb = q.shape[0]
    rows_q = q.shape[2]
    n_new = rows_q // HEADS_PER_GROUP
    lens, offs = _dil_sample_layout(past_len, n_new)
    d2 = DILATIONS[2]
    win_rows = cache_k.shape[1]
    assert lens == WINDOWS and n_new <= d2 and win_rows % d2 == 0 and offs[2] % d2 == 0
    head_rows = offs[2]
    phase_rows = lens[2] // d2
    ck = cache_k.reshape(nb, win_rows // d2, d2, HEADS_PER_GROUP, HEAD_DIM_B)
    cv = cache_v.reshape(nb, win_rows // d2, d2, HEADS_PER_GROUP, HEAD_DIM_B)
    span = [WINDOWS[g] // DILATIONS[g] for g in range(N_GROUPS)]
    pos2 = np.concatenate([t + d2 * np.arange(phase_rows) for t in range(n_new)])
    biases = [_dil_sample_bias(np.arange(lens[0]), lens[0], DILATIONS[0], span[0], n_new),
              _dil_sample_bias(np.arange(lens[1]), lens[1], DILATIONS[1], span[1], n_new),
              _dil_sample_bias(pos2, lens[2], d2, span[2], n_new)]
    bias_new = np.stack([_dil_sample_bias(lens[g] + np.arange(n_new), lens[g], DILATIONS[g], span[g], n_new)
                         for g in range(N_GROUPS)])
    blk = (1, N_GROUPS, rows_q, HEAD_DIM_B)
    per_b = pl.BlockSpec(blk, lambda b: (b, 0, 0, 0))
    buf_a = (2, head_rows // d2, d2, HEADS_PER_GROUP, HEAD_DIM_B)
    buf_b = (2, n_new, phase_rows, HEADS_PER_GROUP, HEAD_DIM_B)
    return pl.pallas_call(
        functools.partial(_dil_sample_kernel, head_rows=head_rows, n_phase=n_new, phase_rows=phase_rows,
                          lens=lens, pos2=None),
        grid=(nb,),
        in_specs=[per_b, per_b, per_b] + [_const_spec(bz.shape) for bz in biases] + [_const_spec(bias_new.shape),
                  pl.BlockSpec(memory_space=pl.ANY), pl.BlockSpec(memory_space=pl.ANY)],
        out_specs=pl.BlockSpec((1, rows_q, HEAD_DIM_B), lambda b: (b, 0, 0)),
        out_shape=jax.ShapeDtypeStruct((nb, rows_q, HEAD_DIM_B), F32),
        scratch_shapes=[pltpu.VMEM(buf_a, F32), pltpu.VMEM(buf_b, F32), pltpu.VMEM(buf_a, F32),
                        pltpu.VMEM(buf_b, F32), pltpu.SemaphoreType.DMA((2, 4))],
        compiler_params=_params("arbitrary"),
        name="dil_sample_attn",
    )(q, k_new, v_new, *[jnp.asarray(bz) for bz in biases], jnp.asarray(bias_new), ck, cv)


def _cache_shift_kernel(k_hbm, v_hbm, kn_hbm, vn_hbm, ko_hbm, vo_hbm, sem, *, lens, n_new):
    cps = []
    off = 0
    for g, ln in enumerate(lens):
        hs = pl.ds(g * HEADS_PER_GROUP, HEADS_PER_GROUP)
        for src, new, dst in ((k_hbm, kn_hbm, ko_hbm), (v_hbm, vn_hbm, vo_hbm)):
            i = len(cps)
            cps.append(pltpu.make_async_copy(src.at[:, pl.ds(off + n_new, ln - n_new)],
                                             dst.at[:, pl.ds(off, ln - n_new)], sem.at[i]))
            cps.append(pltpu.make_async_copy(new.at[:, :, hs], dst.at[:, pl.ds(off + ln - n_new, n_new)],
                                             sem.at[i + 1]))
        off += ln
    for cp in cps:
        cp.start()
    for cp in cps:
        cp.wait()


def _cache_shift(cache_k, cache_v, k_new, v_new, lens):
    n_new = k_new.shape[1]
    anyspec = pl.BlockSpec(memory_space=pl.ANY)
    return pl.pallas_call(
        functools.partial(_cache_shift_kernel, lens=lens, n_new=n_new),
        in_specs=[anyspec] * 4,
        out_specs=[anyspec] * 2,
        out_shape=[jax.ShapeDtypeStruct(cache_k.shape, cache_k.dtype)] * 2,
        scratch_shapes=[pltpu.SemaphoreType.DMA((4 * N_GROUPS,))],
        name="cache_shift",
    )(cache_k, cache_v, k_new, v_new)


def _mla_weights(w_in, g_q, g_kv, w_uq, w_uk, w_uv, w_o):
    w_in_p = jnp.pad(w_in, ((0, 0), (0, 4 * LANES - w_in.shape[1]))).astype(BF16)
    uq = w_uq.reshape(Q_LORA, N_HEADS_A, NOPE_A + ROPE_A)
    w_nope = uq[:, :, :NOPE_A].reshape(Q_LORA, N_HEADS_A * NOPE_A).astype(BF16)
    w_rope = jnp.pad(uq[:, :, NOPE_A:], ((0, 0), (0, 0), (0, LANES - ROPE_A)))
    w_rope = w_rope.reshape(Q_LORA, N_HEADS_A * LANES).astype(BF16)
    ukt = w_uk.transpose(1, 2, 0).reshape(N_HEADS_A // 2, 2, NOPE_A, KV_LORA)
    z = jnp.zeros_like(ukt[:, 0])
    uk_bd = jnp.concatenate([jnp.concatenate([ukt[:, 0], z], axis=2),
                             jnp.concatenate([z, ukt[:, 1]], axis=2)], axis=1).astype(BF16)
    uvt = w_uv.transpose(1, 0, 2).reshape(N_HEADS_A // 2, 2, KV_LORA, V_DIM_A)
    z = jnp.zeros_like(uvt[:, 0])
    uv_bd = jnp.concatenate([jnp.concatenate([uvt[:, 0], z], axis=2),
                             jnp.concatenate([z, uvt[:, 1]], axis=2)], axis=1).astype(BF16)
    return dict(w_in=w_in_p, g_q=g_q[None], g_kv=g_kv[None], w_nope=w_nope, w_rope=w_rope, w_uk=uk_bd,
                w_uv=uv_bd, w_o=w_o.astype(BF16))


def kernel(x_prompt, x_sample, cache_mla_latent, cache_mla_rope, cache_win_k, cache_win_v, state_conv, page_table,
           mla_w_in, mla_g_q, mla_g_kv, mla_w_uq, mla_w_uk, mla_w_uv, mla_w_o,
           dil_w_kv, dil_w_q, dil_w_o,
           ffn_w_in, ffn_conv_w, ffn_conv_b, ffn_w_out,
           ln_mix_g, ln_mix_b, ln_ffn_g, ln_ffn_b):
    batch, seq, _ = x_prompt.shape
    nb, nt, _ = x_sample.shape
    past_len = page_table.shape[1] * PAGE_SIZE
    pos_p = jnp.arange(seq, dtype=jnp.int32)
    pos_s = jnp.repeat(past_len + jnp.arange(nt, dtype=jnp.int32), nb)
    tab_a_p = _rope_tables(pos_p, ROPE_A, LANES)
    tab_a_s = _rope_tables(pos_s, ROPE_A, LANES)
    tab_b_p = _rope_tables(pos_p, ROT_DIM_B, HEAD_DIM_B)
    tab_b_s = _rope_tables(pos_s, ROT_DIM_B, HEAD_DIM_B)

    xp = x_prompt.reshape(batch * seq, D_MODEL)
    xs = x_sample.transpose(1, 0, 2).reshape(nt * nb, D_MODEL)
    ms = nt * nb
    tm_p, tm_s = 256, 128

    def to_batch_major(a):
        return a.reshape(nt, nb, a.shape[-1]).transpose(1, 0, 2)

    lat_p, rope_p, lat_s, rope_s, conv_p, conv_s = [], [], [], [], [], []
    for layer in range(DEPTH):
        g_mix, b_mix = ln_mix_g[layer][None], ln_mix_b[layer][None]
        if layer < N_A_LAYERS:
            wa = _mla_weights(mla_w_in[layer], mla_g_q[layer], mla_g_kv[layer], mla_w_uq[layer], mla_w_uk[layer],
                              mla_w_uv[layer], mla_w_o[layer])
            ckv, kr, kcat, q = _mla_proj(xp, wa, tab_a_p, tm_p)
            ol = _mla_flash(q, kcat, batch, seq)
            xp = _mla_out(ol, wa, xp, g_mix, b_mix, tm_p)
            lat_p.append(ckv.reshape(batch, seq, KV_LORA))
            rope_p.append(kr[:, :ROPE_A].reshape(batch, seq, ROPE_A))

            ckv_s, kr_s, _, q_s = _mla_proj(xs, wa, tab_a_s, tm_s)
            q_b = q_s.reshape(nt, N_HEADS_A, nb, MLA_QK_W).transpose(2, 0, 1, 3).reshape(nb, nt * N_HEADS_A, MLA_QK_W)
            padrows = ((0, 0), (0, SUBLANES - nt), (0, 0))
            c_new = jnp.pad(to_batch_major(ckv_s), padrows)
            r_new = jnp.pad(to_batch_major(kr_s), padrows)
            o_b = _mla_sample_attn(page_table, q_b, c_new, r_new, cache_mla_latent[layer], cache_mla_rope[layer])
            ol_s = o_b.reshape(nb, nt, N_HEADS_A * KV_LORA).transpose(1, 0, 2).reshape(ms, N_HEADS_A * KV_LORA)
            xs = _mla_out(ol_s, wa, xs, g_mix, b_mix, tm_s)
            lat_s.append(to_batch_major(ckv_s))
            rope_s.append(to_batch_major(kr_s[:, :ROPE_A]))
        else:
            if layer == N_A_LAYERS:
                w_kv = dil_w_kv.astype(BF16)
                kv_p = _proj_rope(xp, w_kv, tab_b_p, QB_W, F32, 512)
                kv4 = kv_p.reshape(batch, seq, 2, N_HEADS_B, HEAD_DIM_B)
                lens_p = tuple(min(w, seq) for w in WINDOWS)
                new_win_k_prompt = jnp.concatenate(
                    [kv4[:, seq - lens_p[g]:, 0, g * HEADS_PER_GROUP:(g + 1) * HEADS_PER_GROUP]
                     for g in range(N_GROUPS)], axis=1)
                new_win_v_prompt = jnp.concatenate(
                    [kv4[:, seq - lens_p[g]:, 1, g * HEADS_PER_GROUP:(g + 1) * HEADS_PER_GROUP]
                     for g in range(N_GROUPS)], axis=1)
                kv_s = _proj_rope(xs, w_kv, tab_b_s, QB_W, F32, tm_s)
                kv_s5 = kv_s.reshape(nt, nb, 2, N_HEADS_B, HEAD_DIM_B)
                k_new = kv_s5[:, :, 0].transpose(1, 0, 2, 3)
                v_new = kv_s5[:, :, 1].transpose(1, 0, 2, 3)
                lens_s = tuple(min(w, past_len) for w in WINDOWS)
                new_win_k_sample, new_win_v_sample = _cache_shift(cache_win_k, cache_win_v, k_new, v_new, lens_s)

                def group_rows(a):
                    a = a.reshape(nb, nt, N_GROUPS, HEADS_PER_GROUP, HEAD_DIM_B)
                    return a.transpose(0, 2, 1, 3, 4).reshape(nb, N_GROUPS, nt * HEADS_PER_GROUP, HEAD_DIM_B)

                k_new_g, v_new_g = group_rows(k_new), group_rows(v_new)
            bi = layer - N_A_LAYERS
            w_q = dil_w_q[bi].astype(BF16)
            w_o = dil_w_o[bi].astype(BF16)
            q_p = _proj_rope(xp, w_q, tab_b_p, QB_W, BF16, 512)
            outs, lses = zip(*[_dil_attn_group(q_p, kv_p, batch, seq, g) for g in range(N_GROUPS)])
            xp = _dil_merge(outs, lses, w_o, xp, g_mix, b_mix, 512)

            q_s = _proj_rope(xs, w_q, tab_b_s, QB_W, BF16, tm_s)
            q_g = group_rows(q_s.reshape(nt, nb, N_HEADS_B, HEAD_DIM_B).transpose(1, 0, 2, 3))
            y_b = _dil_sample_attn(q_g, k_new_g, v_new_g, cache_win_k, cache_win_v, past_len)
            y_s = y_b.reshape(nb, nt, GROUP_W).transpose(1, 0, 2).reshape(ms, GROUP_W)
            xs = _proj_ln(y_s, w_o, xs, g_mix, b_mix, tm_s)

        wf = dict(w_in=ffn_w_in[layer].astype(BF16), conv_w=ffn_conv_w[layer], conv_b=ffn_conv_b[layer][None],
                  w_out=ffn_w_out[layer].astype(BF16))
        g_ffn, b_ffn = ln_ffn_g[layer][None], ln_ffn_b[layer][None]
        xp, st_p = _ffn_prompt(xp, wf, g_ffn, b_ffn, seq, 512)
        xs, st_s = _ffn_sample(xs, state_conv[layer].transpose(1, 0, 2), wf, g_ffn, b_ffn, nb, nt)
        conv_p.append(st_p)
        conv_s.append(st_s.transpose(1, 0, 2))

    y_prompt = xp.reshape(batch, seq, D_MODEL)
    y_sample = xs.reshape(nt, nb, D_MODEL).transpose(1, 0, 2)
    return (y_prompt, y_sample, jnp.stack(lat_p, 0), jnp.stack(rope_p, 0), new_win_k_prompt, new_win_v_prompt,
            jnp.stack(conv_p, 0), jnp.stack(lat_s, 0), jnp.stack(rope_s, 0), new_win_k_sample, new_win_v_sample,
            jnp.stack(conv_s, 0))
```

```python
import functools

import numpy as np
import jax
import jax.numpy as jnp
from jax import lax
from jax.experimental import pallas as pl
from jax.experimental.pallas import tpu as pltpu

F32 = jnp.float32
BF16 = jnp.bfloat16

D_MODEL = 1024
DEPTH = 4
PAGE_SIZE = 128
N_A_LAYERS = DEPTH // 2
N_HEADS_A = 16
NOPE_A = 64
ROPE_A = 32
V_DIM_A = 64
Q_LORA = 256
KV_LORA = 128
MLA_SCALE = (NOPE_A + ROPE_A) ** -0.5
WINDOWS = (128, 512, 2048)
DILATIONS = (1, 4, 16)
N_GROUPS = 3
HEADS_PER_GROUP = 8
HEAD_DIM_B = 64
N_HEADS_B = N_GROUPS * HEADS_PER_GROUP
ROT_DIM_B = HEAD_DIM_B // 4
B_SCALE = HEAD_DIM_B ** -0.5
BLOCK_B = 128
ROPE_THETA = 500000.0
D_FF = 2816
CONV_W = 3
ALPHA = (2.0 * DEPTH) ** 0.25
LN_EPS = 1e-5
RMS_EPS = 1e-6
NEG_INF = -1e30
SQRT_HALF = 0.7071067811865476
LOG2_E = 1.4426950408889634
MLA_Q_SCALE = MLA_SCALE * LOG2_E

LANES = 128
SUBLANES = 8
VMEM_LIMIT_BYTES = 48 * 1024 * 1024

GROUP_W = HEADS_PER_GROUP * HEAD_DIM_B
QB_W = N_HEADS_B * HEAD_DIM_B
MLA_QK_W = 2 * LANES
FFN_CHUNK = 256


def _params(*sem):
    return pltpu.CompilerParams(dimension_semantics=sem, vmem_limit_bytes=VMEM_LIMIT_BYTES)


def _const_spec(shape):
    nd = len(shape)
    return pl.BlockSpec(shape, lambda *_: (0,) * nd)


def _dot(a, b):
    return jnp.dot(a, b, preferred_element_type=F32)


def _dot_nt(a, b):
    return lax.dot_general(a, b, (((1,), (1,)), ((), ())), preferred_element_type=F32)


def _layer_norm(z, g, b):
    mu = jnp.mean(z, axis=-1, keepdims=True)
    zc = z - mu
    var = jnp.mean(zc * zc, axis=-1, keepdims=True)
    return zc * lax.rsqrt(var + LN_EPS) * g + b


def _rms_norm(a, g):
    return a * lax.rsqrt(jnp.mean(a * a, axis=-1, keepdims=True) + RMS_EPS) * g


def _rope_lanes(v, cos, sin_a, sin_b, half):
    return (v * cos + pltpu.roll(v, LANES - half, 1) * sin_a + pltpu.roll(v, half, 1) * sin_b)


def _rope_tables(pos, rot_dim, period):
    half = rot_dim // 2
    inv = ROPE_THETA ** (-jnp.arange(half, dtype=F32) * 2.0 / rot_dim)
    ang = pos.astype(F32)[:, None] * inv[None, :]
    cos, sin = jnp.cos(ang), jnp.sin(ang)
    t = pos.shape[0]
    ones = jnp.ones((t, period - rot_dim), F32)
    zr = lambda n: jnp.zeros((t, n), F32)
    c = jnp.concatenate([cos, cos, ones], axis=1)
    sa = jnp.concatenate([-sin, zr(period - half)], axis=1)
    sb = jnp.concatenate([zr(half), sin, zr(period - rot_dim)], axis=1)
    rep = LANES // period
    return tuple(jnp.tile(a, (1, rep)) for a in (c, sa, sb))


def _mla_proj_kernel(x_ref, win_ref, gq_ref, gkv_ref, cos_ref, sa_ref, sb_ref, wn_ref, wr_ref, wuk_ref,
                     ckv_ref, kr_ref, kcat_ref, q_ref, *, tm):
    xb = x_ref[...].astype(BF16)
    a = _dot(xb, win_ref[...])
    cq = _rms_norm(a[:, :Q_LORA], gq_ref[...])
    ckv = _rms_norm(a[:, Q_LORA:Q_LORA + KV_LORA], gkv_ref[...])
    cos, sa, sb = cos_ref[...], sa_ref[...], sb_ref[...]
    kr = _rope_lanes(a[:, Q_LORA + KV_LORA:], cos, sa, sb, ROPE_A // 2)
    ckv_ref[...] = ckv
    kr_ref[...] = kr
    kcat_ref[:, :LANES] = ckv.astype(BF16)
    kcat_ref[:, LANES:] = kr.astype(BF16)
    cqb = cq.astype(BF16)
    qn = _dot(cqb, wn_ref[...])
    qr = _dot(cqb, wr_ref[...])
    for p in range(N_HEADS_A // 2):
        ql2 = _dot(qn[:, p * LANES:(p + 1) * LANES].astype(BF16), wuk_ref[p])
        for hh in range(2):
            h = 2 * p + hh
            ql = (ql2[:, hh * LANES:(hh + 1) * LANES] * MLA_Q_SCALE).astype(BF16)
            rr = _rope_lanes(qr[:, h * LANES:(h + 1) * LANES], cos, sa, sb, ROPE_A // 2)
            rr = (rr * MLA_Q_SCALE).astype(BF16)
            for j in range(tm // LANES):
                q_ref[j, h, :, :LANES] = ql[j * LANES:(j + 1) * LANES]
                q_ref[j, h, :, LANES:] = rr[j * LANES:(j + 1) * LANES]


def _mla_proj(x, w, tables, tm):
    m = x.shape[0]
    cos, sa, sb = tables
    nper = cos.shape[0] // tm
    tab_spec = pl.BlockSpec((tm, LANES), lambda i: (i % nper, 0))
    row = lambda n: pl.BlockSpec((tm, n), lambda i: (i, 0))
    return pl.pallas_call(
        functools.partial(_mla_proj_kernel, tm=tm),
        grid=(m // tm,),
        in_specs=[row(D_MODEL), _const_spec(w["w_in"].shape), _const_spec((1, Q_LORA)), _const_spec((1, KV_LORA)),
                  tab_spec, tab_spec, tab_spec, _const_spec(w["w_nope"].shape), _const_spec(w["w_rope"].shape),
                  _const_spec(w["w_uk"].shape)],
        out_specs=[row(KV_LORA), row(LANES), row(MLA_QK_W),
                   pl.BlockSpec((tm // LANES, N_HEADS_A, LANES, MLA_QK_W), lambda i: (i, 0, 0, 0))],
        out_shape=[jax.ShapeDtypeStruct((m, KV_LORA), F32), jax.ShapeDtypeStruct((m, LANES), F32),
                   jax.ShapeDtypeStruct((m, MLA_QK_W), BF16),
                   jax.ShapeDtypeStruct((m // LANES, N_HEADS_A, LANES, MLA_QK_W), BF16)],
        compiler_params=_params("parallel"),
        name="mla_proj",
    )(x, w["w_in"], w["g_q"], w["g_kv"], cos, sa, sb, w["w_nope"], w["w_rope"], w["w_uk"])


MLA_TQ = 128
MLA_TK = 256
MLA_HEADS_PER_DOT = 8


def _mla_flash_kernel(qi_ref, kj_ref, q_ref, k_ref, o_ref, m_ref, acc_ref):
    t = pl.program_id(1)
    qi = qi_ref[t]
    kj = kj_ref[t]
    last = (qi * MLA_TQ) // MLA_TK

    @pl.when(kj == 0)
    def _():
        m_ref[...] = jnp.full(m_ref.shape, NEG_INF, F32)
        acc_ref[...] = jnp.zeros(acc_ref.shape, F32)

    k = k_ref[...]
    v_ones = jnp.concatenate([k[:, :KV_LORA], jnp.ones((MLA_TK, LANES), BF16)], axis=1)

    nrow = MLA_HEADS_PER_DOT * MLA_TQ

    def step(masked):
        if masked:
            q_pos = qi * MLA_TQ + (lax.broadcasted_iota(jnp.int32, (nrow, MLA_TK), 0) & (MLA_TQ - 1))
            k_pos = kj * MLA_TK + lax.broadcasted_iota(jnp.int32, (nrow, MLA_TK), 1)
            visible = k_pos <= q_pos
        for hg in range(N_HEADS_A // MLA_HEADS_PER_DOT):
            rows = pl.ds(hg * nrow, nrow)
            q = q_ref[0, hg * MLA_HEADS_PER_DOT:(hg + 1) * MLA_HEADS_PER_DOT].reshape(nrow, MLA_QK_W)
            s = _dot_nt(q, k)
            if masked:
                s = jnp.where(visible, s, NEG_INF)
            m_prev = m_ref[rows, :]
            m_new = jnp.maximum(m_prev, jnp.max(s, axis=-1, keepdims=True))
            alpha = jnp.exp2(m_prev - m_new)
            p = jnp.exp2(s - jnp.concatenate([m_new] * (MLA_TK // LANES), axis=1))
            pv = _dot(p.astype(BF16), v_ones)
            acc_ref[rows, :] = jnp.concatenate([alpha, alpha], axis=1) * acc_ref[rows, :] + pv
            m_ref[rows, :] = m_new

    @pl.when(kj != last)
    def _():
        step(False)

    @pl.when(kj == last)
    def _():
        step(True)
        for h in range(N_HEADS_A):
            a = acc_ref[pl.ds(h * MLA_TQ, MLA_TQ), :]
            o_ref[:, h * KV_LORA:(h + 1) * KV_LORA] = (a[:, :KV_LORA] / a[:, KV_LORA:]).astype(BF16)


def _mla_flash(q, kcat, batch, seq):
    nq = seq // MLA_TQ
    nk = seq // MLA_TK
    pairs = [(i, j) for i in range(nq) for j in range((i * MLA_TQ) // MLA_TK + 1)]
    qi = jnp.asarray(np.array([p[0] for p in pairs], np.int32))
    kj = jnp.asarray(np.array([p[1] for p in pairs], np.int32))
    rows = N_HEADS_A * MLA_TQ
    return pl.pallas_call(
        _mla_flash_kernel,
        grid_spec=pltpu.PrefetchScalarGridSpec(
            num_scalar_prefetch=2,
            grid=(batch, len(pairs)),
            in_specs=[pl.BlockSpec((1, N_HEADS_A, MLA_TQ, MLA_QK_W), lambda b, t, qi, kj: (b * nq + qi[t], 0, 0, 0)),
                      pl.BlockSpec((MLA_TK, MLA_QK_W), lambda b, t, qi, kj: (b * nk + kj[t], 0))],
            out_specs=pl.BlockSpec((MLA_TQ, N_HEADS_A * KV_LORA), lambda b, t, qi, kj: (b * nq + qi[t], 0)),
            scratch_shapes=[pltpu.VMEM((rows, LANES), F32), pltpu.VMEM((rows, KV_LORA + LANES), F32)],
        ),
        out_shape=jax.ShapeDtypeStruct((batch * seq, N_HEADS_A * KV_LORA), BF16),
        compiler_params=_params("parallel", "arbitrary"),
        name="mla_flash",
    )(qi, kj, q, kcat)


def _mla_sample_kernel(pt_ref, q_ref, cn_ref, rn_ref, lat_hbm, rope_hbm, o_ref, lat_buf, rope_buf, sem,
                       *, layer, n_pages, n_new):
    b = pl.program_id(0)
    nb = pl.num_programs(0)
    slot = b % 2

    def lat_copy(bb, sl, j):
        return pltpu.make_async_copy(lat_hbm.at[layer, pt_ref[bb, j]], lat_buf.at[sl, j], sem.at[sl, 0])

    def rope_copy(bb, sl, j):
        return pltpu.make_async_copy(rope_hbm.at[layer, pt_ref[bb, j]],
                                     rope_buf.at[sl, :, pl.ds(j * PAGE_SIZE, PAGE_SIZE)], sem.at[sl, 1])

    def start_all(bb, sl):
        for j in range(n_pages):
            lat_copy(bb, sl, j).start()
            rope_copy(bb, sl, j).start()

    @pl.when(b == 0)
    def _():
        start_all(0, 0)

    @pl.when(b + 1 < nb)
    def _():
        start_all(b + 1, 1 - slot)

    for j in range(n_pages):
        lat_copy(b, slot, j).wait()
        rope_copy(b, slot, j).wait()

    n_keys = n_pages * PAGE_SIZE
    lat = lat_buf[slot].reshape(n_keys, KV_LORA).astype(BF16)
    rope_t = rope_buf[slot].astype(BF16)
    q = q_ref[0]
    q_lat = q[:, :KV_LORA]
    cn = cn_ref[0].astype(BF16)
    rn = rn_ref[0].astype(BF16)
    s = _dot_nt(q_lat, lat) + _dot(q[:, KV_LORA:KV_LORA + ROPE_A], rope_t)
    s_new = _dot_nt(q_lat, cn) + _dot_nt(q[:, KV_LORA:], rn)
    t_q = lax.broadcasted_iota(jnp.int32, s_new.shape, 0) // N_HEADS_A
    j_k = lax.broadcasted_iota(jnp.int32, s_new.shape, 1)
    s_new = jnp.where(j_k <= t_q, s_new, NEG_INF)
    m = jnp.maximum(jnp.max(s, axis=-1, keepdims=True), jnp.max(s_new, axis=-1, keepdims=True))
    e = jnp.exp2(s - m)
    e_new = jnp.exp2(s_new - m)
    den = jnp.sum(e, axis=-1, keepdims=True) + jnp.sum(e_new, axis=-1, keepdims=True)
    out = _dot((e / den).astype(BF16), lat) + _dot((e_new / den).astype(BF16), cn)
    o_ref[0] = out.astype(BF16)


def _mla_sample_attn(page_table, q, c_new, r_new, cache_lat, cache_rope_t, layer):
    nb, n_pages = page_table.shape
    rows = q.shape[1]
    n_new = rows // N_HEADS_A
    return pl.pallas_call(
        functools.partial(_mla_sample_kernel, layer=layer, n_pages=n_pages, n_new=n_new),
        grid_spec=pltpu.PrefetchScalarGridSpec(
            num_scalar_prefetch=1,
            grid=(nb,),
            in_specs=[pl.BlockSpec((1, rows, MLA_QK_W), lambda b, pt: (b, 0, 0)),
                      pl.BlockSpec((1, SUBLANES, KV_LORA), lambda b, pt: (b, 0, 0)),
                      pl.BlockSpec((1, SUBLANES, LANES), lambda b, pt: (b, 0, 0)),
                      pl.BlockSpec(memory_space=pl.ANY),
                      pl.BlockSpec(memory_space=pl.ANY)],
            out_specs=pl.BlockSpec((1, rows, KV_LORA), lambda b, pt: (b, 0, 0)),
            scratch_shapes=[pltpu.VMEM((2, n_pages, PAGE_SIZE, KV_LORA), F32),
                            pltpu.VMEM((2, ROPE_A, n_pages * PAGE_SIZE), F32),
                            pltpu.SemaphoreType.DMA((2, 2))],
        ),
        out_shape=jax.ShapeDtypeStruct((nb, rows, KV_LORA), BF16),
        compiler_params=_params("arbitrary"),
        name="mla_sample_attn",
    )(page_table, q, c_new, r_new, cache_lat, cache_rope_t)


def _mla_out_kernel(ol_ref, wuv_ref, wo_ref, x_ref, g_ref, b_ref, y_ref):
    parts = [_dot(ol_ref[:, p * 2 * KV_LORA:(p + 1) * 2 * KV_LORA], wuv_ref[p]) for p in range(N_HEADS_A // 2)]
    o = jnp.concatenate(parts, axis=1).astype(BF16)
    mix = _dot(o, wo_ref[...])
    y_ref[...] = _layer_norm(ALPHA * x_ref[...] + mix, g_ref[...], b_ref[...])


def _mla_out(ol, w, x, g, b, tm):
    m = x.shape[0]
    row = lambda n: pl.BlockSpec((tm, n), lambda i: (i, 0))
    return pl.pallas_call(
        _mla_out_kernel,
        grid=(m // tm,),
        in_specs=[row(N_HEADS_A * KV_LORA), _const_spec(w["w_uv"].shape), _const_spec(w["w_o"].shape),
                  row(D_MODEL), _const_spec((1, D_MODEL)), _const_spec((1, D_MODEL))],
        out_specs=row(D_MODEL),
        out_shape=jax.ShapeDtypeStruct((m, D_MODEL), F32),
        compiler_params=_params("parallel"),
        name="mla_out",
    )(ol, w["w_uv"], w["w_o"], x, g, b)


def _proj_ln_kernel(a_ref, wo_ref, x_ref, g_ref, b_ref, y_ref):
    mix = _dot(a_ref[...].astype(BF16), wo_ref[...])
    y_ref[...] = _layer_norm(ALPHA * x_ref[...] + mix, g_ref[...], b_ref[...])


def _proj_ln(a, wo, x, g, b, tm):
    m = x.shape[0]
    row = lambda n: pl.BlockSpec((tm, n), lambda i: (i, 0))
    return pl.pallas_call(
        _proj_ln_kernel,
        grid=(m // tm,),
        in_specs=[row(a.shape[1]), _const_spec(wo.shape), row(D_MODEL), _const_spec((1, D_MODEL)),
                  _const_spec((1, D_MODEL))],
        out_specs=row(D_MODEL),
        out_shape=jax.ShapeDtypeStruct((m, D_MODEL), F32),
        compiler_params=_params("parallel"),
        name="proj_ln",
    )(a, wo, x, g, b)


def _dil_merge_kernel(o0, o1, o2, l0, l1, l2, wo_ref, x_ref, g_ref, b_ref, y_ref):
    wide = lambda ref: jnp.concatenate([ref[j] for j in range(ref.shape[0])], axis=1)
    la, lb, lc = wide(l0), wide(l1), wide(l2)
    m = jnp.maximum(jnp.maximum(la, lb), lc)
    ea, eb, ec = jnp.exp(la - m), jnp.exp(lb - m), jnp.exp(lc - m)
    den = ea + eb + ec
    y = (ea / den) * wide(o0) + (eb / den) * wide(o1) + (ec / den) * wide(o2)
    mix = _dot(y.astype(BF16), wo_ref[...])
    y_ref[...] = _layer_norm(ALPHA * x_ref[...] + mix, g_ref[...], b_ref[...])


def _dil_merge(outs, lses, wo, x, g, b, tm):
    m = x.shape[0]
    row = lambda n: pl.BlockSpec((tm, n), lambda i: (i, 0))
    return pl.pallas_call(
        _dil_merge_kernel,
        grid=(m // tm,),
        in_specs=[pl.BlockSpec((outs[0].shape[0], tm, LANES), lambda i: (0, i, 0))] * 6
        + [_const_spec(wo.shape), row(D_MODEL), _const_spec((1, D_MODEL)), _const_spec((1, D_MODEL))],
        out_specs=row(D_MODEL),
        out_shape=jax.ShapeDtypeStruct((m, D_MODEL), F32),
        compiler_params=_params("parallel"),
        name="dil_merge",
    )(*outs, *lses, wo, x, g, b)


def _gelu_gate(acc, val):
    return 0.5 * acc * (1.0 + lax.erf(acc * SQRT_HALF)) * val


def _ffn_prompt_kernel(x_ref, xh_ref, win_ref, cw_ref, cb_ref, wout_ref, g_ref, b_ref, y_ref, st_ref,
                       *, tm, tiles_per_seq):
    i = pl.program_id(0)
    x = x_ref[...]
    xb = x.astype(BF16)
    xhb = xh_ref[...].astype(BF16)
    keep = ((i % tiles_per_seq) != 0).astype(F32)
    row = lax.broadcasted_iota(jnp.int32, (tm, FFN_CHUNK), 0)
    acc = jnp.zeros((tm, D_MODEL), F32)
    for c in range(D_FF // FFN_CHUNK):
        lo = c * FFN_CHUNK
        wg = win_ref[:, lo:lo + FFN_CHUNK]
        gate = _dot(xb, wg)
        val = _dot(xb, win_ref[:, D_FF + lo:D_FF + lo + FFN_CHUNK])
        gh = _dot(xhb, wg) * keep
        g1 = jnp.where(row == 0, gh[SUBLANES - 1:SUBLANES], pltpu.roll(gate, 1, 0))
        g2 = jnp.where(row == 0, gh[SUBLANES - 2:SUBLANES - 1],
                       jnp.where(row == 1, gh[SUBLANES - 1:SUBLANES], pltpu.roll(gate, 2, 0)))
        a = cb_ref[:, lo:lo + FFN_CHUNK] + cw_ref[0:1, lo:lo + FFN_CHUNK] * g2
        a = a + cw_ref[1:2, lo:lo + FFN_CHUNK] * g1
        a = a + cw_ref[2:3, lo:lo + FFN_CHUNK] * gate
        h = _gelu_gate(a, val).astype(BF16)
        acc = acc + _dot(h, wout_ref[lo:lo + FFN_CHUNK, :])
        st_ref[0, :, lo:lo + FFN_CHUNK] = gate[tm - SUBLANES:]
    y_ref[...] = _layer_norm(ALPHA * x + acc, g_ref[...], b_ref[...])


def _ffn_prompt(x, w, g, b, seq, tm):
    m = x.shape[0]
    tiles_per_seq = seq // tm
    hb = tm // SUBLANES
    row = lambda n: pl.BlockSpec((tm, n), lambda i: (i, 0))
    y, st = pl.pallas_call(
        functools.partial(_ffn_prompt_kernel, tm=tm, tiles_per_seq=tiles_per_seq),
        grid=(m // tm,),
        in_specs=[row(D_MODEL),
                  pl.BlockSpec((SUBLANES, D_MODEL), lambda i: (jnp.maximum(i * hb - 1, 0), 0)),
                  _const_spec(w["w_in"].shape), _const_spec((CONV_W, D_FF)), _const_spec((1, D_FF)),
                  _const_spec(w["w_out"].shape), _const_spec((1, D_MODEL)), _const_spec((1, D_MODEL))],
        out_specs=[row(D_MODEL), pl.BlockSpec((1, SUBLANES, D_FF), lambda i: (i // tiles_per_seq, 0, 0))],
        out_shape=[jax.ShapeDtypeStruct((m, D_MODEL), F32),
                   jax.ShapeDtypeStruct((m // seq, SUBLANES, D_FF), F32)],
        compiler_params=_params("arbitrary"),
        name="ffn_prompt",
    )(x, x, w["w_in"], w["conv_w"], w["conv_b"], w["w_out"], g, b)
    return y, st[:, SUBLANES - (CONV_W - 1):]


def _ffn_sample_kernel(x_ref, s_ref, win_ref, cw_ref, cb_ref, wout_ref, g_ref, b_ref, y_ref, st_ref, *, nb, nt):
    x = x_ref[...]
    xb = x.astype(BF16)
    acc = jnp.zeros((nt * nb, D_MODEL), F32)
    for c in range(D_FF // FFN_CHUNK):
        lo = c * FFN_CHUNK
        gate = _dot(xb, win_ref[:, lo:lo + FFN_CHUNK])
        val = _dot(xb, win_ref[:, D_FF + lo:D_FF + lo + FFN_CHUNK])
        s0 = s_ref[0, :, lo:lo + FFN_CHUNK]
        s1 = s_ref[1, :, lo:lo + FFN_CHUNK]
        g1 = jnp.concatenate([s1, gate[:(nt - 1) * nb]], axis=0)
        g2 = jnp.concatenate([s0, s1, gate[:(nt - 2) * nb]], axis=0)
        a = cb_ref[:, lo:lo + FFN_CHUNK] + cw_ref[0:1, lo:lo + FFN_CHUNK] * g2
        a = a + cw_ref[1:2, lo:lo + FFN_CHUNK] * g1
        a = a + cw_ref[2:3, lo:lo + FFN_CHUNK] * gate
        h = _gelu_gate(a, val).astype(BF16)
        acc = acc + _dot(h, wout_ref[lo:lo + FFN_CHUNK, :])
        st_ref[0, :, lo:lo + FFN_CHUNK] = gate[(nt - 2) * nb:(nt - 1) * nb]
        st_ref[1, :, lo:lo + FFN_CHUNK] = gate[(nt - 1) * nb:]
    y_ref[...] = _layer_norm(ALPHA * x + acc, g_ref[...], b_ref[...])


def _ffn_sample(x, state, w, g, b, nb, nt):
    m = x.shape[0]
    return pl.pallas_call(
        functools.partial(_ffn_sample_kernel, nb=nb, nt=nt),
        out_shape=[jax.ShapeDtypeStruct((m, D_MODEL), F32), jax.ShapeDtypeStruct((CONV_W - 1, nb, D_FF), F32)],
        compiler_params=pltpu.CompilerParams(vmem_limit_bytes=VMEM_LIMIT_BYTES),
        name="ffn_sample",
    )(x, state, w["w_in"], w["conv_w"], w["conv_b"], w["w_out"], g, b)


PROJ_CHUNK = 512


def _proj_rope_kernel(x_ref, w_ref, cos_ref, sa_ref, sb_ref, o_ref, *, n, rope_cols, lane_blocks):
    xb = x_ref[...].astype(BF16)
    cos, sa, sb = cos_ref[...], sa_ref[...], sb_ref[...]
    for c in range(n // PROJ_CHUNK):
        lo = c * PROJ_CHUNK
        y = _dot(xb, w_ref[:, lo:lo + PROJ_CHUNK])
        for k in range(PROJ_CHUNK // LANES):
            yk = y[:, k * LANES:(k + 1) * LANES]
            if lo < rope_cols:
                yk = _rope_lanes(yk, cos, sa, sb, ROT_DIM_B // 2)
            if lane_blocks:
                o_ref[lo // LANES + k] = yk.astype(o_ref.dtype)
            else:
                o_ref[:, lo + k * LANES:lo + (k + 1) * LANES] = yk.astype(o_ref.dtype)


def _proj_rope(x, w, tables, rope_cols, out_dtype, tm, lane_blocks=False):
    m = x.shape[0]
    n = w.shape[1]
    cos, sa, sb = tables
    nper = cos.shape[0] // tm
    tab_spec = pl.BlockSpec((tm, LANES), lambda i: (i % nper, 0))
    if lane_blocks:
        out_spec = pl.BlockSpec((n // LANES, tm, LANES), lambda i: (0, i, 0))
        out_shape = jax.ShapeDtypeStruct((n // LANES, m, LANES), out_dtype)
    else:
        out_spec = pl.BlockSpec((tm, n), lambda i: (i, 0))
        out_shape = jax.ShapeDtypeStruct((m, n), out_dtype)
    return pl.pallas_call(
        functools.partial(_proj_rope_kernel, n=n, rope_cols=rope_cols, lane_blocks=lane_blocks),
        grid=(m // tm,),
        in_specs=[pl.BlockSpec((tm, D_MODEL), lambda i: (i, 0)), _const_spec(w.shape), tab_spec, tab_spec, tab_spec],
        out_specs=out_spec,
        out_shape=out_shape,
        compiler_params=_params("parallel"),
        name="proj_rope",
    )(x, w, cos, sa, sb)


GROUP_SLABS = GROUP_W // LANES


def _dil_attn_kernel(q_ref, k_ref, v_ref, o_ref, l_ref, *, d, span):
    seq = q_ref.shape[2]
    nblk = seq // (BLOCK_B * d)
    has_prev = nblk > 1
    nk = 2 * BLOCK_B if has_prev else BLOCK_B
    a = lax.broadcasted_iota(jnp.int32, (BLOCK_B, nk), 0)
    c = lax.broadcasted_iota(jnp.int32, (BLOCK_B, nk), 1)
    diff = a + (nk - BLOCK_B) - c
    in_band = (diff >= 0) & (diff <= span)

    r_low_count = min(d, SUBLANES)

    def phase_rows(ref, n, r):
        r_hi, r_lo = r
        start = pl.multiple_of(n * (BLOCK_B * d) + r_hi * SUBLANES, SUBLANES) + r_lo
        return jnp.concatenate([ref[j, 0, pl.ds(start, BLOCK_B, stride=d), :] for j in range(GROUP_SLABS)], axis=1)

    def body(it, carry, r_lo):
        r = (it // nblk, r_lo)
        n = it % nblk
        q = phase_rows(q_ref, n, r).astype(BF16)
        if has_prev:
            n_prev = jnp.maximum(n - 1, 0)
            k = jnp.concatenate([phase_rows(k_ref, n_prev, r), phase_rows(k_ref, n, r)], axis=0).astype(BF16)
            v = jnp.concatenate([phase_rows(v_ref, n_prev, r), phase_rows(v_ref, n, r)], axis=0).astype(BF16)
            mask = in_band & ((n - 1) * BLOCK_B + c >= 0)
        else:
            k = phase_rows(k_ref, n, r).astype(BF16)
            v = phase_rows(v_ref, n, r).astype(BF16)
            mask = in_band
        outs, lses = [], []
        for h in range(HEADS_PER_GROUP):
            hs = slice(h * HEAD_DIM_B, (h + 1) * HEAD_DIM_B)
            s = _dot_nt(q[:, hs], k[:, hs]) * B_SCALE
            s = jnp.where(mask, s, NEG_INF)
            m = jnp.max(s, axis=-1, keepdims=True)
            e = jnp.exp(s - m)
            den = jnp.sum(e, axis=-1, keepdims=True)
            outs.append(_dot((e / den).astype(BF16), v[:, hs]))
            lses.append(jnp.broadcast_to(m + jnp.log(den), (BLOCK_B, HEAD_DIM_B)))
        start = pl.multiple_of(n * (BLOCK_B * d) + r[0] * SUBLANES, SUBLANES) + r_lo
        rows = pl.ds(start, BLOCK_B, stride=d)
        for j in range(GROUP_SLABS):
            o_ref[j, 0, rows, :] = jnp.concatenate(outs[2 * j:2 * j + 2], axis=1)
            l_ref[j, 0, rows, :] = jnp.concatenate(lses[2 * j:2 * j + 2], axis=1)
        return carry

    for r_lo in range(r_low_count):
        lax.fori_loop(0, (d // r_low_count) * nblk, functools.partial(body, r_lo=r_lo), 0)


def _dil_attn_group(q, kv, batch, seq, g):
    d = DILATIONS[g]
    span = WINDOWS[g] // d
    nq = QB_W // GROUP_W
    blk = (GROUP_SLABS, 1, seq, LANES)
    col = lambda j: pl.BlockSpec(blk, lambda b: (j, b, 0, 0))
    shape = jax.ShapeDtypeStruct((GROUP_SLABS, batch, seq, LANES), F32)
    q4 = q.reshape(q.shape[0], batch, seq, LANES)
    kv4 = kv.reshape(kv.shape[0], batch, seq, LANES)
    o, l = pl.pallas_call(
        functools.partial(_dil_attn_kernel, d=d, span=span),
        grid=(batch,),
        in_specs=[col(g), col(g), col(nq + g)],
        out_specs=[col(0), col(0)],
        out_shape=[shape, shape],
        compiler_params=_params("parallel"),
        name=f"dil_attn_g{g}",
    )(q4, kv4, kv4)
    return o.reshape(GROUP_SLABS, batch * seq, LANES), l.reshape(GROUP_SLABS, batch * seq, LANES)


DS_HEADS = 4
DS_ROWS = 4 * SUBLANES


def _dil_sample_bias(lens, n_new):
    bias = np.full((DS_ROWS, sum(lens)), NEG_INF, np.float32)
    bias_new = np.full((DS_ROWS, LANES), NEG_INF, np.float32)
    off = 0
    for g, ln in enumerate(lens):
        d = DILATIONS[g]
        span = WINDOWS[g] // d
        for t in range(n_new):
            rel = ln + t - np.arange(ln)
            ok = (rel % d == 0) & (rel // d <= span)
            bias[g * SUBLANES + t, off:off + ln] = np.where(ok, 0.0, NEG_INF)
            for t2 in range(t + 1):
                if (t - t2) % d == 0 and (t - t2) // d <= span:
                    bias_new[g * SUBLANES + t, n_new * g + t2] = 0.0
        off += ln
    return bias, bias_new


def _shift_append(src, new_t, dst_ref, hh, lens, n_new):
    nblk = src.shape[1] // LANES
    ends = {}
    off = 0
    for g, ln in enumerate(lens):
        off += ln
        ends[off // LANES - 1] = g
    lane = lax.broadcasted_iota(jnp.int32, (src.shape[0], LANES), 1)
    rolled = [pltpu.roll(src[:, cb * LANES:(cb + 1) * LANES], LANES - n_new, 1) for cb in range(nblk)]
    for cb in range(nblk):
        if cb in ends:
            tail = pltpu.roll(new_t, LANES - n_new - n_new * ends[cb], 1)
        else:
            tail = rolled[cb + 1]
        dst_ref[0, hh, :, cb * LANES:(cb + 1) * LANES] = jnp.where(lane < LANES - n_new, rolled[cb], tail)


def _dil_sample_kernel(q_ref, kn_ref, vn_ref, bias_ref, bn_ref, k_ref, v_ref, *out_refs, lens, n_new, write_cache):
    y_ref = out_refs[0]
    bias = bias_ref[...]
    bias_new = bn_ref[...]
    for hh in range(DS_HEADS):
        kf, vf = k_ref[0, hh], v_ref[0, hh]
        knf, vnf = kn_ref[0, hh], vn_ref[0, hh]
        q = q_ref[0, hh]
        s = _dot(q, kf.astype(BF16)) * B_SCALE + bias
        s_new = _dot(q, knf.astype(BF16)) * B_SCALE + bias_new
        m = jnp.maximum(jnp.max(s, axis=-1, keepdims=True), jnp.max(s_new, axis=-1, keepdims=True))
        e = jnp.exp(s - m)
        e_new = jnp.exp(s_new - m)
        den = jnp.sum(e, axis=-1, keepdims=True) + jnp.sum(e_new, axis=-1, keepdims=True)
        o = _dot_nt((e / den).astype(BF16), vf.astype(BF16)) + _dot_nt((e_new / den).astype(BF16), vnf.astype(BF16))
        lse = m + jnp.log(den)
        grp = lambda a, g: a[g * SUBLANES:(g + 1) * SUBLANES]
        mm = jnp.maximum(jnp.maximum(grp(lse, 0), grp(lse, 1)), grp(lse, 2))
        ws = [jnp.exp(grp(lse, g) - mm) for g in range(N_GROUPS)]
        wsum = ws[0] + ws[1] + ws[2]
        y_ref[0, hh] = sum((ws[g] / wsum) * grp(o, g) for g in range(N_GROUPS))
        if write_cache:
            _shift_append(kf, knf, out_refs[1], hh, lens, n_new)
            _shift_append(vf, vnf, out_refs[2], hh, lens, n_new)


def _dil_sample(q, kn_t, vn_t, ck_t, cv_t, lens, n_new, write_cache):
    nb, hp, hd, rows = ck_t.shape
    assert all(ln % LANES == 0 for ln in lens) and sum(lens) == rows and N_GROUPS * n_new <= LANES
    bias, bias_new = _dil_sample_bias(lens, n_new)
    cache_spec = pl.BlockSpec((1, DS_HEADS, hd, rows), lambda b, j: (b, j, 0, 0))
    new_spec = pl.BlockSpec((1, DS_HEADS, hd, LANES), lambda b, j: (b, j, 0, 0))
    q_spec = pl.BlockSpec((1, DS_HEADS, DS_ROWS, hd), lambda b, j: (b, j, 0, 0))
    y_spec = pl.BlockSpec((1, DS_HEADS, SUBLANES, hd), lambda b, j: (b, j, 0, 0))
    out_specs = [y_spec]
    out_shape = [jax.ShapeDtypeStruct((nb, hp, SUBLANES, hd), F32)]
    if write_cache:
        out_specs += [cache_spec, cache_spec]
        out_shape += [jax.ShapeDtypeStruct(ck_t.shape, F32)] * 2
    return pl.pallas_call(
        functools.partial(_dil_sample_kernel, lens=lens, n_new=n_new, write_cache=write_cache),
        grid=(nb, hp // DS_HEADS),
        in_specs=[q_spec, new_spec, new_spec, _const_spec(bias.shape), _const_spec(bias_new.shape),
                  cache_spec, cache_spec],
        out_specs=out_specs,
        out_shape=out_shape,
        compiler_params=_params("parallel", "parallel"),
        name="dil_sample_update" if write_cache else "dil_sample",
    )(q, kn_t, vn_t, jnp.asarray(bias), jnp.asarray(bias_new), ck_t, cv_t)


def _mla_weights(w_in, g_q, g_kv, w_uq, w_uk, w_uv, w_o):
    w_in_p = jnp.pad(w_in, ((0, 0), (0, 4 * LANES - w_in.shape[1]))).astype(BF16)
    uq = w_uq.reshape(Q_LORA, N_HEADS_A, NOPE_A + ROPE_A)
    w_nope = uq[:, :, :NOPE_A].reshape(Q_LORA, N_HEADS_A * NOPE_A).astype(BF16)
    w_rope = jnp.pad(uq[:, :, NOPE_A:], ((0, 0), (0, 0), (0, LANES - ROPE_A)))
    w_rope = w_rope.reshape(Q_LORA, N_HEADS_A * LANES).astype(BF16)
    ukt = w_uk.transpose(1, 2, 0).reshape(N_HEADS_A // 2, 2, NOPE_A, KV_LORA)
    z = jnp.zeros_like(ukt[:, 0])
    uk_bd = jnp.concatenate([jnp.concatenate([ukt[:, 0], z], axis=2),
                             jnp.concatenate([z, ukt[:, 1]], axis=2)], axis=1).astype(BF16)
    uvt = w_uv.transpose(1, 0, 2).reshape(N_HEADS_A // 2, 2, KV_LORA, V_DIM_A)
    z = jnp.zeros_like(uvt[:, 0])
    uv_bd = jnp.concatenate([jnp.concatenate([uvt[:, 0], z], axis=2),
                             jnp.concatenate([z, uvt[:, 1]], axis=2)], axis=1).astype(BF16)
    return dict(w_in=w_in_p, g_q=g_q[None], g_kv=g_kv[None], w_nope=w_nope, w_rope=w_rope, w_uk=uk_bd,
                w_uv=uv_bd, w_o=w_o.astype(BF16))


def kernel(x_prompt, x_sample, cache_mla_latent, cache_mla_rope, cache_win_k, cache_win_v, state_conv, page_table,
           mla_w_in, mla_g_q, mla_g_kv, mla_w_uq, mla_w_uk, mla_w_uv, mla_w_o,
           dil_w_kv, dil_w_q, dil_w_o,
           ffn_w_in, ffn_conv_w, ffn_conv_b, ffn_w_out,
           ln_mix_g, ln_mix_b, ln_ffn_g, ln_ffn_b):
    batch, seq, _ = x_prompt.shape
    nb, nt, _ = x_sample.shape
    past_len = page_table.shape[1] * PAGE_SIZE
    pos_p = jnp.arange(seq, dtype=jnp.int32)
    pos_s = jnp.repeat(past_len + jnp.arange(nt, dtype=jnp.int32), nb)
    tab_a_p = _rope_tables(pos_p, ROPE_A, LANES)
    tab_a_s = _rope_tables(pos_s, ROPE_A, LANES)
    tab_b_p = _rope_tables(pos_p, ROT_DIM_B, HEAD_DIM_B)
    tab_b_s = _rope_tables(pos_s, ROT_DIM_B, HEAD_DIM_B)

    xp = x_prompt.reshape(batch * seq, D_MODEL)
    xs = x_sample.transpose(1, 0, 2).reshape(nt * nb, D_MODEL)
    ms = nt * nb
    tm_p, tm_s = 256, 128
    cache_rope_t = cache_mla_rope.transpose(0, 1, 3, 2)

    def to_batch_major(a):
        return a.reshape(nt, nb, a.shape[-1]).transpose(1, 0, 2)

    lat_p, rope_p, lat_s, rope_s, conv_p, conv_s = [], [], [], [], [], []
    for layer in range(DEPTH):
        g_mix, b_mix = ln_mix_g[layer][None], ln_mix_b[layer][None]
        if layer < N_A_LAYERS:
            wa = _mla_weights(mla_w_in[layer], mla_g_q[layer], mla_g_kv[layer], mla_w_uq[layer], mla_w_uk[layer],
                              mla_w_uv[layer], mla_w_o[layer])
            ckv, kr, kcat, q = _mla_proj(xp, wa, tab_a_p, tm_p)
            ol = _mla_flash(q, kcat, batch, seq)
            xp = _mla_out(ol, wa, xp, g_mix, b_mix, tm_p)
            lat_p.append(ckv.reshape(batch, seq, KV_LORA))
            rope_p.append(kr[:, :ROPE_A].reshape(batch, seq, ROPE_A))

            ckv_s, kr_s, _, q_s = _mla_proj(xs, wa, tab_a_s, tm_s)
            q_b = q_s.reshape(nt, N_HEADS_A, nb, MLA_QK_W).transpose(2, 0, 1, 3).reshape(nb, nt * N_HEADS_A, MLA_QK_W)
            padrows = ((0, 0), (0, SUBLANES - nt), (0, 0))
            c_new = jnp.pad(to_batch_major(ckv_s), padrows)
            r_new = jnp.pad(to_batch_major(kr_s), padrows)
            o_b = _mla_sample_attn(page_table, q_b, c_new, r_new, cache_mla_latent, cache_rope_t, layer)
            ol_s = o_b.reshape(nb, nt, N_HEADS_A * KV_LORA).transpose(1, 0, 2).reshape(ms, N_HEADS_A * KV_LORA)
            xs = _mla_out(ol_s, wa, xs, g_mix, b_mix, tm_s)
            lat_s.append(to_batch_major(ckv_s))
            rope_s.append(to_batch_major(kr_s[:, :ROPE_A]))
        else:
            if layer == N_A_LAYERS:
                w_kv = dil_w_kv.astype(BF16)
                kv_p = _proj_rope(xp, w_kv, tab_b_p, QB_W, F32, 512, lane_blocks=True)
                kv4 = kv_p.reshape(2, N_HEADS_B // 2, batch, seq, 2, HEAD_DIM_B).transpose(2, 3, 0, 1, 4, 5)
                kv4 = kv4.reshape(batch, seq, 2, N_HEADS_B, HEAD_DIM_B)
                lens_p = tuple(min(w, seq) for w in WINDOWS)
                new_win_k_prompt = jnp.concatenate(
                    [kv4[:, seq - lens_p[g]:, 0, g * HEADS_PER_GROUP:(g + 1) * HEADS_PER_GROUP]
                     for g in range(N_GROUPS)], axis=1)
                new_win_v_prompt = jnp.concatenate(
                    [kv4[:, seq - lens_p[g]:, 1, g * HEADS_PER_GROUP:(g + 1) * HEADS_PER_GROUP]
                     for g in range(N_GROUPS)], axis=1)
                kv_s = _proj_rope(xs, w_kv, tab_b_s, QB_W, F32, tm_s)
                kv_s6 = kv_s.reshape(nt, nb, 2, N_GROUPS, HEADS_PER_GROUP, HEAD_DIM_B)
                lens_s = tuple(min(w, past_len) for w in WINDOWS)

                def new_rows_t(a):
                    a = a.transpose(1, 3, 4, 2, 0).reshape(nb, HEADS_PER_GROUP, HEAD_DIM_B, N_GROUPS * nt)
                    return jnp.pad(a, ((0, 0), (0, 0), (0, 0), (0, LANES - N_GROUPS * nt)))

                kn_t, vn_t = new_rows_t(kv_s6[:, :, 0]), new_rows_t(kv_s6[:, :, 1])
                ck_t = cache_win_k.transpose(0, 2, 3, 1)
                cv_t = cache_win_v.transpose(0, 2, 3, 1)
            bi = layer - N_A_LAYERS
            w_q = dil_w_q[bi].astype(BF16)
            w_o = dil_w_o[bi].astype(BF16)
            q_p = _proj_rope(xp, w_q, tab_b_p, QB_W, F32, 512, lane_blocks=True)
            outs, lses = zip(*[_dil_attn_group(q_p, kv_p, batch, seq, g) for g in range(N_GROUPS)])
            xp = _dil_merge(outs, lses, w_o, xp, g_mix, b_mix, 512)

            q_s = _proj_rope(xs, w_q, tab_b_s, QB_W, BF16, tm_s)
            q_g = q_s.reshape(nt, nb, N_GROUPS, HEADS_PER_GROUP, HEAD_DIM_B).transpose(1, 3, 2, 0, 4)
            q_g = jnp.pad(q_g, ((0, 0), (0, 0), (0, DS_ROWS // SUBLANES - N_GROUPS), (0, SUBLANES - nt), (0, 0)))
            q_g = q_g.reshape(nb, HEADS_PER_GROUP, DS_ROWS, HEAD_DIM_B)
            write_cache = layer == N_A_LAYERS
            res = _dil_sample(q_g, kn_t, vn_t, ck_t, cv_t, lens_s, nt, write_cache)
            if write_cache:
                new_win_k_sample = res[1].transpose(0, 3, 1, 2)
                new_win_v_sample = res[2].transpose(0, 3, 1, 2)
            y_s = res[0][:, :, :nt].transpose(2, 0, 1, 3).reshape(ms, GROUP_W)
            xs = _proj_ln(y_s, w_o, xs, g_mix, b_mix, tm_s)

        wf = dict(w_in=ffn_w_in[layer].astype(BF16), conv_w=ffn_conv_w[layer], conv_b=ffn_conv_b[layer][None],
                  w_out=ffn_w_out[layer].astype(BF16))
        g_ffn, b_ffn = ln_ffn_g[layer][None], ln_ffn_b[layer][None]
        xp, st_p = _ffn_prompt(xp, wf, g_ffn, b_ffn, seq, 512)
        xs, st_s = _ffn_sample(xs, state_conv[layer].transpose(1, 0, 2), wf, g_ffn, b_ffn, nb, nt)
        conv_p.append(st_p)
        conv_s.append(st_s.transpose(1, 0, 2))

    y_prompt = xp.reshape(batch, seq, D_MODEL)
    y_sample = xs.reshape(nt, nb, D_MODEL).transpose(1, 0, 2)
    return (y_prompt, y_sample, jnp.stack(lat_p, 0), jnp.stack(rope_p, 0), new_win_k_prompt, new_win_v_prompt,
            jnp.stack(conv_p, 0), jnp.stack(lat_s, 0), jnp.stack(rope_s, 0), new_win_k_sample, new_win_v_sample,
            jnp.stack(conv_s, 0))
```

```python
import functools

import numpy as np
import jax
import jax.numpy as jnp
from jax import lax
from jax.experimental import pallas as pl
from jax.experimental.pallas import tpu as pltpu

F32 = jnp.float32
BF16 = jnp.bfloat16

D_MODEL = 1024
DEPTH = 4
PAGE_SIZE = 128
N_A_LAYERS = DEPTH // 2
N_HEADS_A = 16
NOPE_A = 64
ROPE_A = 32
V_DIM_A = 64
Q_LORA = 256
KV_LORA = 128
MLA_SCALE = (NOPE_A + ROPE_A) ** -0.5
WINDOWS = (128, 512, 2048)
DILATIONS = (1, 4, 16)
N_GROUPS = 3
HEADS_PER_GROUP = 8
HEAD_DIM_B = 64
N_HEADS_B = N_GROUPS * HEADS_PER_GROUP
ROT_DIM_B = HEAD_DIM_B // 4
B_SCALE = HEAD_DIM_B ** -0.5
BLOCK_B = 128
ROPE_THETA = 500000.0
D_FF = 2816
CONV_W = 3
ALPHA = (2.0 * DEPTH) ** 0.25
LN_EPS = 1e-5
RMS_EPS = 1e-6
NEG_INF = -1e30
SQRT_HALF = 0.7071067811865476
LOG2_E = 1.4426950408889634
MLA_Q_SCALE = MLA_SCALE * LOG2_E

LANES = 128
SUBLANES = 8
VMEM_LIMIT_BYTES = 48 * 1024 * 1024

GROUP_W = HEADS_PER_GROUP * HEAD_DIM_B
QB_W = N_HEADS_B * HEAD_DIM_B
MLA_QK_W = 2 * LANES
FFN_CHUNK = 256


def _params(*sem):
    return pltpu.CompilerParams(dimension_semantics=sem, vmem_limit_bytes=VMEM_LIMIT_BYTES)


def _const_spec(shape):
    nd = len(shape)
    return pl.BlockSpec(shape, lambda *_: (0,) * nd)


def _dot(a, b):
    return jnp.dot(a, b, preferred_element_type=F32)


def _dot_nt(a, b):
    return lax.dot_general(a, b, (((1,), (1,)), ((), ())), preferred_element_type=F32)


def _layer_norm(z, g, b):
    mu = jnp.mean(z, axis=-1, keepdims=True)
    zc = z - mu
    var = jnp.mean(zc * zc, axis=-1, keepdims=True)
    return zc * lax.rsqrt(var + LN_EPS) * g + b


def _rms_norm(a, g):
    return a * lax.rsqrt(jnp.mean(a * a, axis=-1, keepdims=True) + RMS_EPS) * g


def _rope_lanes(v, cos, sin_a, sin_b, half):
    return (v * cos + pltpu.roll(v, LANES - half, 1) * sin_a + pltpu.roll(v, half, 1) * sin_b)


def _rope_tables(pos, rot_dim, period):
    half = rot_dim // 2
    inv = ROPE_THETA ** (-jnp.arange(half, dtype=F32) * 2.0 / rot_dim)
    ang = pos.astype(F32)[:, None] * inv[None, :]
    cos, sin = jnp.cos(ang), jnp.sin(ang)
    t = pos.shape[0]
    ones = jnp.ones((t, period - rot_dim), F32)
    zr = lambda n: jnp.zeros((t, n), F32)
    c = jnp.concatenate([cos, cos, ones], axis=1)
    sa = jnp.concatenate([-sin, zr(period - half)], axis=1)
    sb = jnp.concatenate([zr(half), sin, zr(period - rot_dim)], axis=1)
    rep = LANES // period
    return tuple(jnp.tile(a, (1, rep)) for a in (c, sa, sb))


def _mla_proj_kernel(x_ref, win_ref, gq_ref, gkv_ref, cos_ref, sa_ref, sb_ref, wn_ref, wr_ref, wuk_ref,
                     ckv_ref, kr_ref, kcat_ref, q_ref, *, tm):
    xb = x_ref[...].astype(BF16)
    a = _dot(xb, win_ref[...])
    cq = _rms_norm(a[:, :Q_LORA], gq_ref[...])
    ckv = _rms_norm(a[:, Q_LORA:Q_LORA + KV_LORA], gkv_ref[...])
    cos, sa, sb = cos_ref[...], sa_ref[...], sb_ref[...]
    kr = _rope_lanes(a[:, Q_LORA + KV_LORA:], cos, sa, sb, ROPE_A // 2)
    ckv_ref[...] = ckv
    kr_ref[...] = kr
    kcat_ref[:, :LANES] = ckv.astype(BF16)
    kcat_ref[:, LANES:] = kr.astype(BF16)
    cqb = cq.astype(BF16)
    qn = _dot(cqb, wn_ref[...])
    qr = _dot(cqb, wr_ref[...])
    for p in range(N_HEADS_A // 2):
        ql2 = _dot(qn[:, p * LANES:(p + 1) * LANES].astype(BF16), wuk_ref[p])
        for hh in range(2):
            h = 2 * p + hh
            ql = (ql2[:, hh * LANES:(hh + 1) * LANES] * MLA_Q_SCALE).astype(BF16)
            rr = _rope_lanes(qr[:, h * LANES:(h + 1) * LANES], cos, sa, sb, ROPE_A // 2)
            rr = (rr * MLA_Q_SCALE).astype(BF16)
            for j in range(tm // LANES):
                q_ref[j, h, :, :LANES] = ql[j * LANES:(j + 1) * LANES]
                q_ref[j, h, :, LANES:] = rr[j * LANES:(j + 1) * LANES]


def _mla_proj(x, w, tables, tm):
    m = x.shape[0]
    cos, sa, sb = tables
    nper = cos.shape[0] // tm
    tab_spec = pl.BlockSpec((tm, LANES), lambda i: (i % nper, 0))
    row = lambda n: pl.BlockSpec((tm, n), lambda i: (i, 0))
    return pl.pallas_call(
        functools.partial(_mla_proj_kernel, tm=tm),
        grid=(m // tm,),
        in_specs=[row(D_MODEL), _const_spec(w["w_in"].shape), _const_spec((1, Q_LORA)), _const_spec((1, KV_LORA)),
                  tab_spec, tab_spec, tab_spec, _const_spec(w["w_nope"].shape), _const_spec(w["w_rope"].shape),
                  _const_spec(w["w_uk"].shape)],
        out_specs=[row(KV_LORA), row(LANES), row(MLA_QK_W),
                   pl.BlockSpec((tm // LANES, N_HEADS_A, LANES, MLA_QK_W), lambda i: (i, 0, 0, 0))],
        out_shape=[jax.ShapeDtypeStruct((m, KV_LORA), F32), jax.ShapeDtypeStruct((m, LANES), F32),
                   jax.ShapeDtypeStruct((m, MLA_QK_W), BF16),
                   jax.ShapeDtypeStruct((m // LANES, N_HEADS_A, LANES, MLA_QK_W), BF16)],
        compiler_params=_params("parallel"),
        name="mla_proj",
    )(x, w["w_in"], w["g_q"], w["g_kv"], cos, sa, sb, w["w_nope"], w["w_rope"], w["w_uk"])


MLA_TQ = 256
MLA_TK = 512
MLA_HEADS_PER_DOT = 8
MLA_SUB = MLA_TQ // LANES


def _mla_flash_kernel(qi_ref, kj_ref, q_ref, k_ref, o_ref, m_ref, acc_ref):
    t = pl.program_id(1)
    qi = qi_ref[t]
    kj = kj_ref[t]
    last = (qi * MLA_TQ + MLA_TQ - 1) // MLA_TK

    @pl.when(kj == 0)
    def _():
        m_ref[...] = jnp.full(m_ref.shape, NEG_INF, F32)
        acc_ref[...] = jnp.zeros(acc_ref.shape, F32)

    k = k_ref[...]
    v_ones = jnp.concatenate([k[:, :KV_LORA], jnp.ones((MLA_TK, LANES), BF16)], axis=1)

    nrow = MLA_HEADS_PER_DOT * LANES
    groups = N_HEADS_A // MLA_HEADS_PER_DOT

    def step(masked):
        for sb in range(MLA_SUB):
            if masked:
                q_pos = qi * MLA_TQ + sb * LANES + (lax.broadcasted_iota(jnp.int32, (nrow, MLA_TK), 0) & (LANES - 1))
                k_pos = kj * MLA_TK + lax.broadcasted_iota(jnp.int32, (nrow, MLA_TK), 1)
                visible = k_pos <= q_pos
            for hg in range(groups):
                rows = pl.ds((sb * groups + hg) * nrow, nrow)
                q = q_ref[sb, hg * MLA_HEADS_PER_DOT:(hg + 1) * MLA_HEADS_PER_DOT].reshape(nrow, MLA_QK_W)
                s = _dot_nt(q, k)
                if masked:
                    s = jnp.where(visible, s, NEG_INF)
                m_prev = m_ref[rows, :]
                m_new = jnp.maximum(m_prev, jnp.max(s, axis=-1, keepdims=True))
                alpha = jnp.exp2(m_prev - m_new)
                p = jnp.exp2(s - jnp.concatenate([m_new] * (MLA_TK // LANES), axis=1))
                pv = _dot(p.astype(BF16), v_ones)
                acc_ref[rows, :] = jnp.concatenate([alpha, alpha], axis=1) * acc_ref[rows, :] + pv
                m_ref[rows, :] = m_new

    @pl.when(kj != last)
    def _():
        step(False)

    @pl.when(kj == last)
    def _():
        step(True)
        for sb in range(MLA_SUB):
            for h in range(N_HEADS_A):
                a = acc_ref[pl.ds((sb * N_HEADS_A + h) * LANES, LANES), :]
                o_ref[sb * LANES:(sb + 1) * LANES, h * KV_LORA:(h + 1) * KV_LORA] = (
                    a[:, :KV_LORA] / a[:, KV_LORA:]).astype(BF16)


def _mla_flash(q, kcat, batch, seq):
    nq = seq // MLA_TQ
    nk = seq // MLA_TK
    pairs = [(i, j) for i in range(nq) for j in range((i * MLA_TQ + MLA_TQ - 1) // MLA_TK + 1)]
    qi = jnp.asarray(np.array([p[0] for p in pairs], np.int32))
    kj = jnp.asarray(np.array([p[1] for p in pairs], np.int32))
    rows = N_HEADS_A * MLA_TQ
    return pl.pallas_call(
        _mla_flash_kernel,
        grid_spec=pltpu.PrefetchScalarGridSpec(
            num_scalar_prefetch=2,
            grid=(batch, len(pairs)),
            in_specs=[pl.BlockSpec((MLA_SUB, N_HEADS_A, LANES, MLA_QK_W), lambda b, t, qi, kj: (b * nq + qi[t], 0, 0, 0)),
                      pl.BlockSpec((MLA_TK, MLA_QK_W), lambda b, t, qi, kj: (b * nk + kj[t], 0))],
            out_specs=pl.BlockSpec((MLA_TQ, N_HEADS_A * KV_LORA), lambda b, t, qi, kj: (b * nq + qi[t], 0)),
            scratch_shapes=[pltpu.VMEM((rows, LANES), F32), pltpu.VMEM((rows, KV_LORA + LANES), F32)],
        ),
        out_shape=jax.ShapeDtypeStruct((batch * seq, N_HEADS_A * KV_LORA), BF16),
        compiler_params=_params("parallel", "arbitrary"),
        name="mla_flash",
    )(qi, kj, q, kcat)


MLA_SAMPLE_SEQS = 2


def _mla_sample_kernel(pt_ref, q_ref, cn_ref, rn_ref, lat_hbm, rope_hbm, o_ref, lat_buf, rope_buf, sem,
                       *, layer, n_pages, n_new):
    step = pl.program_id(0)
    n_steps = pl.num_programs(0)
    slot = step % 2
    seqs = range(MLA_SAMPLE_SEQS)

    def lat_copy(st, sl, u, j):
        page = pt_ref[st * MLA_SAMPLE_SEQS + u, j]
        return pltpu.make_async_copy(lat_hbm.at[layer, page], lat_buf.at[sl, u, j], sem.at[sl, 0])

    def rope_copy(st, sl, u, j):
        page = pt_ref[st * MLA_SAMPLE_SEQS + u, j]
        return pltpu.make_async_copy(rope_hbm.at[layer, page],
                                     rope_buf.at[sl, u, :, pl.ds(j * PAGE_SIZE, PAGE_SIZE)], sem.at[sl, 1])

    def start_all(st, sl):
        for u in seqs:
            for j in range(n_pages):
                lat_copy(st, sl, u, j).start()
                rope_copy(st, sl, u, j).start()

    @pl.when(step == 0)
    def _():
        start_all(0, 0)

    @pl.when(step + 1 < n_steps)
    def _():
        start_all(step + 1, 1 - slot)

    for u in seqs:
        for j in range(n_pages):
            lat_copy(step, slot, u, j).wait()
            rope_copy(step, slot, u, j).wait()

    n_keys = n_pages * PAGE_SIZE
    lats, cns, scores = [], [], []
    for u in seqs:
        lat = lat_buf[slot, u].reshape(n_keys, KV_LORA).astype(BF16)
        rope_t = rope_buf[slot, u].astype(BF16)
        q = q_ref[u]
        q_lat = q[:, :KV_LORA]
        cn = cn_ref[u].astype(BF16)
        rn = rn_ref[u].astype(BF16)
        s = _dot_nt(q_lat, lat) + _dot(q[:, KV_LORA:KV_LORA + ROPE_A], rope_t)
        s_new = _dot_nt(q_lat, cn) + _dot_nt(q[:, KV_LORA:], rn)
        lats.append(lat)
        cns.append(cn)
        scores.append((s, s_new))
    probs = []
    for s, s_new in scores:
        t_q = lax.broadcasted_iota(jnp.int32, s_new.shape, 0) // N_HEADS_A
        j_k = lax.broadcasted_iota(jnp.int32, s_new.shape, 1)
        s_new = jnp.where(j_k <= t_q, s_new, NEG_INF)
        m = jnp.maximum(jnp.max(s, axis=-1, keepdims=True), jnp.max(s_new, axis=-1, keepdims=True))
        e = jnp.exp2(s - m)
        e_new = jnp.exp2(s_new - m)
        den = jnp.sum(e, axis=-1, keepdims=True) + jnp.sum(e_new, axis=-1, keepdims=True)
        probs.append(((e / den).astype(BF16), (e_new / den).astype(BF16)))
    for u in seqs:
        o_ref[u] = (_dot(probs[u][0], lats[u]) + _dot(probs[u][1], cns[u])).astype(BF16)


def _mla_sample_attn(page_table, q, c_new, r_new, cache_lat, cache_rope_t, layer):
    nb, n_pages = page_table.shape
    rows = q.shape[1]
    n_new = rows // N_HEADS_A
    ns = MLA_SAMPLE_SEQS
    assert nb % ns == 0
    return pl.pallas_call(
        functools.partial(_mla_sample_kernel, layer=layer, n_pages=n_pages, n_new=n_new),
        grid_spec=pltpu.PrefetchScalarGridSpec(
            num_scalar_prefetch=1,
            grid=(nb // ns,),
            in_specs=[pl.BlockSpec((ns, rows, MLA_QK_W), lambda b, pt: (b, 0, 0)),
                      pl.BlockSpec((ns, SUBLANES, KV_LORA), lambda b, pt: (b, 0, 0)),
                      pl.BlockSpec((ns, SUBLANES, LANES), lambda b, pt: (b, 0, 0)),
                      pl.BlockSpec(memory_space=pl.ANY),
                      pl.BlockSpec(memory_space=pl.ANY)],
            out_specs=pl.BlockSpec((ns, rows, KV_LORA), lambda b, pt: (b, 0, 0)),
            scratch_shapes=[pltpu.VMEM((2, ns, n_pages, PAGE_SIZE, KV_LORA), F32),
                            pltpu.VMEM((2, ns, ROPE_A, n_pages * PAGE_SIZE), F32),
                            pltpu.SemaphoreType.DMA((2, 2))],
        ),
        out_shape=jax.ShapeDtypeStruct((nb, rows, KV_LORA), BF16),
        compiler_params=_params("arbitrary"),
        name="mla_sample_attn",
    )(page_table, q, c_new, r_new, cache_lat, cache_rope_t)


def _mla_out_kernel(ol_ref, wuv_ref, wo_ref, x_ref, g_ref, b_ref, y_ref):
    parts = [_dot(ol_ref[:, p * 2 * KV_LORA:(p + 1) * 2 * KV_LORA], wuv_ref[p]) for p in range(N_HEADS_A // 2)]
    o = jnp.concatenate(parts, axis=1).astype(BF16)
    mix = _dot(o, wo_ref[...])
    y_ref[...] = _layer_norm(ALPHA * x_ref[...] + mix, g_ref[...], b_ref[...])


def _mla_out(ol, w, x, g, b, tm):
    m = x.shape[0]
    row = lambda n: pl.BlockSpec((tm, n), lambda i: (i, 0))
    return pl.pallas_call(
        _mla_out_kernel,
        grid=(m // tm,),
        in_specs=[row(N_HEADS_A * KV_LORA), _const_spec(w["w_uv"].shape), _const_spec(w["w_o"].shape),
                  row(D_MODEL), _const_spec((1, D_MODEL)), _const_spec((1, D_MODEL))],
        out_specs=row(D_MODEL),
        out_shape=jax.ShapeDtypeStruct((m, D_MODEL), F32),
        compiler_params=_params("parallel"),
        name="mla_out",
    )(ol, w["w_uv"], w["w_o"], x, g, b)


def _proj_ln_kernel(a_ref, wo_ref, x_ref, g_ref, b_ref, y_ref):
    mix = _dot(a_ref[...].astype(BF16), wo_ref[...])
    y_ref[...] = _layer_norm(ALPHA * x_ref[...] + mix, g_ref[...], b_ref[...])


def _proj_ln(a, wo, x, g, b, tm):
    m = x.shape[0]
    row = lambda n: pl.BlockSpec((tm, n), lambda i: (i, 0))
    return pl.pallas_call(
        _proj_ln_kernel,
        grid=(m // tm,),
        in_specs=[row(a.shape[1]), _const_spec(wo.shape), row(D_MODEL), _const_spec((1, D_MODEL)),
                  _const_spec((1, D_MODEL))],
        out_specs=row(D_MODEL),
        out_shape=jax.ShapeDtypeStruct((m, D_MODEL), F32),
        compiler_params=_params("parallel"),
        name="proj_ln",
    )(a, wo, x, g, b)


def _dil_merge_kernel(o0, o1, o2, l0, l1, l2, wo_ref, x_ref, g_ref, b_ref, y_ref):
    wide = lambda ref: jnp.concatenate([ref[j] for j in range(ref.shape[0])], axis=1)
    la, lb, lc = wide(l0), wide(l1), wide(l2)
    m = jnp.maximum(jnp.maximum(la, lb), lc)
    ea, eb, ec = jnp.exp(la - m), jnp.exp(lb - m), jnp.exp(lc - m)
    den = ea + eb + ec
    y = (ea / den) * wide(o0) + (eb / den) * wide(o1) + (ec / den) * wide(o2)
    mix = _dot(y.astype(BF16), wo_ref[...])
    y_ref[...] = _layer_norm(ALPHA * x_ref[...] + mix, g_ref[...], b_ref[...])


def _dil_merge(outs, lses, wo, x, g, b, tm):
    m = x.shape[0]
    row = lambda n: pl.BlockSpec((tm, n), lambda i: (i, 0))
    return pl.pallas_call(
        _dil_merge_kernel,
        grid=(m // tm,),
        in_specs=[pl.BlockSpec((outs[0].shape[0], tm, LANES), lambda i: (0, i, 0))] * 6
        + [_const_spec(wo.shape), row(D_MODEL), _const_spec((1, D_MODEL)), _const_spec((1, D_MODEL))],
        out_specs=row(D_MODEL),
        out_shape=jax.ShapeDtypeStruct((m, D_MODEL), F32),
        compiler_params=_params("parallel"),
        name="dil_merge",
    )(*outs, *lses, wo, x, g, b)


def _gelu_gate(acc, val):
    return 0.5 * acc * (1.0 + lax.erf(acc * SQRT_HALF)) * val


def _ffn_prompt_kernel(x_ref, xh_ref, win_ref, cw_ref, cb_ref, wout_ref, g_ref, b_ref, y_ref, st_ref,
                       *, tm, tiles_per_seq):
    i = pl.program_id(0)
    x = x_ref[...]
    xb = x.astype(BF16)
    xhb = xh_ref[...].astype(BF16)
    keep = ((i % tiles_per_seq) != 0).astype(F32)
    row = lax.broadcasted_iota(jnp.int32, (tm, FFN_CHUNK), 0)
    acc = jnp.zeros((tm, D_MODEL), F32)
    n_chunks = D_FF // FFN_CHUNK

    def up_proj(c):
        lo = c * FFN_CHUNK
        wg = win_ref[:, lo:lo + FFN_CHUNK]
        return (_dot(xb, wg), _dot(xb, win_ref[:, D_FF + lo:D_FF + lo + FFN_CHUNK]), _dot(xhb, wg))

    nxt = up_proj(0)
    for c in range(n_chunks):
        lo = c * FFN_CHUNK
        gate, val, gh = nxt
        if c + 1 < n_chunks:
            nxt = up_proj(c + 1)
        gh = gh * keep
        g1 = jnp.where(row == 0, gh[SUBLANES - 1:SUBLANES], pltpu.roll(gate, 1, 0))
        g2 = jnp.where(row == 0, gh[SUBLANES - 2:SUBLANES - 1],
                       jnp.where(row == 1, gh[SUBLANES - 1:SUBLANES], pltpu.roll(gate, 2, 0)))
        a = cb_ref[:, lo:lo + FFN_CHUNK] + cw_ref[0:1, lo:lo + FFN_CHUNK] * g2
        a = a + cw_ref[1:2, lo:lo + FFN_CHUNK] * g1
        a = a + cw_ref[2:3, lo:lo + FFN_CHUNK] * gate
        h = _gelu_gate(a, val).astype(BF16)
        acc = acc + _dot(h, wout_ref[lo:lo + FFN_CHUNK, :])
        st_ref[0, :, lo:lo + FFN_CHUNK] = gate[tm - SUBLANES:]
    y_ref[...] = _layer_norm(ALPHA * x + acc, g_ref[...], b_ref[...])


def _ffn_prompt(x, w, g, b, seq, tm):
    m = x.shape[0]
    tiles_per_seq = seq // tm
    hb = tm // SUBLANES
    row = lambda n: pl.BlockSpec((tm, n), lambda i: (i, 0))
    y, st = pl.pallas_call(
        functools.partial(_ffn_prompt_kernel, tm=tm, tiles_per_seq=tiles_per_seq),
        grid=(m // tm,),
        in_specs=[row(D_MODEL),
                  pl.BlockSpec((SUBLANES, D_MODEL), lambda i: (jnp.maximum(i * hb - 1, 0), 0)),
                  _const_spec(w["w_in"].shape), _const_spec((CONV_W, D_FF)), _const_spec((1, D_FF)),
                  _const_spec(w["w_out"].shape), _const_spec((1, D_MODEL)), _const_spec((1, D_MODEL))],
        out_specs=[row(D_MODEL), pl.BlockSpec((1, SUBLANES, D_FF), lambda i: (i // tiles_per_seq, 0, 0))],
        out_shape=[jax.ShapeDtypeStruct((m, D_MODEL), F32),
                   jax.ShapeDtypeStruct((m // seq, SUBLANES, D_FF), F32)],
        compiler_params=_params("arbitrary"),
        name="ffn_prompt",
    )(x, x, w["w_in"], w["conv_w"], w["conv_b"], w["w_out"], g, b)
    return y, st[:, SUBLANES - (CONV_W - 1):]


def _ffn_sample_kernel(x_ref, s_ref, win_ref, cw_ref, cb_ref, wout_ref, g_ref, b_ref, y_ref, st_ref, *, nb, nt):
    x = x_ref[...]
    xb = x.astype(BF16)
    acc = jnp.zeros((nt * nb, D_MODEL), F32)
    for c in range(D_FF // FFN_CHUNK):
        lo = c * FFN_CHUNK
        gate = _dot(xb, win_ref[:, lo:lo + FFN_CHUNK])
        val = _dot(xb, win_ref[:, D_FF + lo:D_FF + lo + FFN_CHUNK])
        s0 = s_ref[0, :, lo:lo + FFN_CHUNK]
        s1 = s_ref[1, :, lo:lo + FFN_CHUNK]
        g1 = jnp.concatenate([s1, gate[:(nt - 1) * nb]], axis=0)
        g2 = jnp.concatenate([s0, s1, gate[:(nt - 2) * nb]], axis=0)
        a = cb_ref[:, lo:lo + FFN_CHUNK] + cw_ref[0:1, lo:lo + FFN_CHUNK] * g2
        a = a + cw_ref[1:2, lo:lo + FFN_CHUNK] * g1
        a = a + cw_ref[2:3, lo:lo + FFN_CHUNK] * gate
        h = _gelu_gate(a, val).astype(BF16)
        acc = acc + _dot(h, wout_ref[lo:lo + FFN_CHUNK, :])
        st_ref[0, :, lo:lo + FFN_CHUNK] = gate[(nt - 2) * nb:(nt - 1) * nb]
        st_ref[1, :, lo:lo + FFN_CHUNK] = gate[(nt - 1) * nb:]
    y_ref[...] = _layer_norm(ALPHA * x + acc, g_ref[...], b_ref[...])


def _ffn_sample(x, state, w, g, b, nb, nt):
    m = x.shape[0]
    return pl.pallas_call(
        functools.partial(_ffn_sample_kernel, nb=nb, nt=nt),
        out_shape=[jax.ShapeDtypeStruct((m, D_MODEL), F32), jax.ShapeDtypeStruct((CONV_W - 1, nb, D_FF), F32)],
        compiler_params=pltpu.CompilerParams(vmem_limit_bytes=VMEM_LIMIT_BYTES),
        name="ffn_sample",
    )(x, state, w["w_in"], w["conv_w"], w["conv_b"], w["w_out"], g, b)


PROJ_CHUNK = 512


def _proj_rope_kernel(x_ref, w_ref, cos_ref, sa_ref, sb_ref, o_ref, *, n, rope_cols, lane_blocks):
    xb = x_ref[...].astype(BF16)
    cos, sa, sb = cos_ref[...], sa_ref[...], sb_ref[...]
    for c in range(n // PROJ_CHUNK):
        lo = c * PROJ_CHUNK
        y = _dot(xb, w_ref[:, lo:lo + PROJ_CHUNK])
        for k in range(PROJ_CHUNK // LANES):
            yk = y[:, k * LANES:(k + 1) * LANES]
            if lo < rope_cols:
                yk = _rope_lanes(yk, cos, sa, sb, ROT_DIM_B // 2)
            if lane_blocks:
                o_ref[lo // LANES + k] = yk.astype(o_ref.dtype)
            else:
                o_ref[:, lo + k * LANES:lo + (k + 1) * LANES] = yk.astype(o_ref.dtype)


def _proj_rope(x, w, tables, rope_cols, out_dtype, tm, lane_blocks=False):
    m = x.shape[0]
    n = w.shape[1]
    cos, sa, sb = tables
    nper = cos.shape[0] // tm
    tab_spec = pl.BlockSpec((tm, LANES), lambda i: (i % nper, 0))
    if lane_blocks:
        out_spec = pl.BlockSpec((n // LANES, tm, LANES), lambda i: (0, i, 0))
        out_shape = jax.ShapeDtypeStruct((n // LANES, m, LANES), out_dtype)
    else:
        out_spec = pl.BlockSpec((tm, n), lambda i: (i, 0))
        out_shape = jax.ShapeDtypeStruct((m, n), out_dtype)
    return pl.pallas_call(
        functools.partial(_proj_rope_kernel, n=n, rope_cols=rope_cols, lane_blocks=lane_blocks),
        grid=(m // tm,),
        in_specs=[pl.BlockSpec((tm, D_MODEL), lambda i: (i, 0)), _const_spec(w.shape), tab_spec, tab_spec, tab_spec],
        out_specs=out_spec,
        out_shape=out_shape,
        compiler_params=_params("parallel"),
        name="proj_rope",
    )(x, w, cos, sa, sb)


GROUP_SLABS = GROUP_W // LANES


def _dil_attn_kernel(q_ref, k_ref, v_ref, o_ref, l_ref, *, d, span):
    seq = q_ref.shape[2]
    nblk = seq // (BLOCK_B * d)
    has_prev = nblk > 1
    nk = 2 * BLOCK_B if has_prev else BLOCK_B
    a = lax.broadcasted_iota(jnp.int32, (BLOCK_B, nk), 0)
    c = lax.broadcasted_iota(jnp.int32, (BLOCK_B, nk), 1)
    diff = a + (nk - BLOCK_B) - c
    in_band = (diff >= 0) & (diff <= span)

    r_low_count = min(d, SUBLANES)

    def phase_rows(ref, n, r):
        r_hi, r_lo = r
        start = pl.multiple_of(n * (BLOCK_B * d) + r_hi * SUBLANES, SUBLANES) + r_lo
        return jnp.concatenate([ref[j, 0, pl.ds(start, BLOCK_B, stride=d), :] for j in range(GROUP_SLABS)], axis=1)

    def body(it, carry, r_lo):
        r = (it // nblk, r_lo)
        n = it % nblk
        q = phase_rows(q_ref, n, r).astype(BF16)
        if has_prev:
            n_prev = jnp.maximum(n - 1, 0)
            k = jnp.concatenate([phase_rows(k_ref, n_prev, r), phase_rows(k_ref, n, r)], axis=0).astype(BF16)
            v = jnp.concatenate([phase_rows(v_ref, n_prev, r), phase_rows(v_ref, n, r)], axis=0).astype(BF16)
            mask = in_band & ((n - 1) * BLOCK_B + c >= 0)
        else:
            k = phase_rows(k_ref, n, r).astype(BF16)
            v = phase_rows(v_ref, n, r).astype(BF16)
            mask = in_band
        head = lambda a, h: a[:, h * HEAD_DIM_B:(h + 1) * HEAD_DIM_B]
        scores = [_dot_nt(head(q, h), head(k, h)) for h in range(HEADS_PER_GROUP)]
        probs, lses = [], []
        for s in scores:
            s = jnp.where(mask, s * B_SCALE, NEG_INF)
            m = jnp.max(s, axis=-1, keepdims=True)
            e = jnp.exp(s - m)
            den = jnp.sum(e, axis=-1, keepdims=True)
            probs.append((e / den).astype(BF16))
            lses.append(jnp.broadcast_to(m + jnp.log(den), (BLOCK_B, HEAD_DIM_B)))
        outs = [_dot(probs[h], head(v, h)) for h in range(HEADS_PER_GROUP)]
        start = pl.multiple_of(n * (BLOCK_B * d) + r[0] * SUBLANES, SUBLANES) + r_lo
        rows = pl.ds(start, BLOCK_B, stride=d)
        for j in range(GROUP_SLABS):
            o_ref[j, 0, rows, :] = jnp.concatenate(outs[2 * j:2 * j + 2], axis=1)
            l_ref[j, 0, rows, :] = jnp.concatenate(lses[2 * j:2 * j + 2], axis=1)
        return carry

    for r_lo in range(r_low_count):
        lax.fori_loop(0, (d // r_low_count) * nblk, functools.partial(body, r_lo=r_lo), 0)


def _dil_attn_group(q, kv, batch, seq, g):
    d = DILATIONS[g]
    span = WINDOWS[g] // d
    nq = QB_W // GROUP_W
    blk = (GROUP_SLABS, 1, seq, LANES)
    col = lambda j: pl.BlockSpec(blk, lambda b: (j, b, 0, 0))
    shape = jax.ShapeDtypeStruct((GROUP_SLABS, batch, seq, LANES), F32)
    q4 = q.reshape(q.shape[0], batch, seq, LANES)
    kv4 = kv.reshape(kv.shape[0], batch, seq, LANES)
    o, l = pl.pallas_call(
        functools.partial(_dil_attn_kernel, d=d, span=span),
        grid=(batch,),
        in_specs=[col(g), col(g), col(nq + g)],
        out_specs=[col(0), col(0)],
        out_shape=[shape, shape],
        compiler_params=_params("parallel"),
        name=f"dil_attn_g{g}",
    )(q4, kv4, kv4)
    return o.reshape(GROUP_SLABS, batch * seq, LANES), l.reshape(GROUP_SLABS, batch * seq, LANES)


DS_HEADS = 4
DS_ROWS = 4 * SUBLANES


def _dil_sample_bias(lens, n_new):
    bias = np.full((DS_ROWS, sum(lens)), NEG_INF, np.float32)
    bias_new = np.full((DS_ROWS, LANES), NEG_INF, np.float32)
    off = 0
    for g, ln in enumerate(lens):
        d = DILATIONS[g]
        span = WINDOWS[g] // d
        for t in range(n_new):
            rel = ln + t - np.arange(ln)
            ok = (rel % d == 0) & (rel // d <= span)
            bias[g * SUBLANES + t, off:off + ln] = np.where(ok, 0.0, NEG_INF)
            for t2 in range(t + 1):
                if (t - t2) % d == 0 and (t - t2) // d <= span:
                    bias_new[g * SUBLANES + t, n_new * g + t2] = 0.0
        off += ln
    return bias, bias_new


def _shift_append(src, new_t, dst_ref, hh, lens, n_new):
    nblk = src.shape[1] // LANES
    ends = {}
    off = 0
    for g, ln in enumerate(lens):
        off += ln
        ends[off // LANES - 1] = g
    lane = lax.broadcasted_iota(jnp.int32, (src.shape[0], LANES), 1)
    rolled = [pltpu.roll(src[:, cb * LANES:(cb + 1) * LANES], LANES - n_new, 1) for cb in range(nblk)]
    for cb in range(nblk):
        if cb in ends:
            tail = pltpu.roll(new_t, LANES - n_new - n_new * ends[cb], 1)
        else:
            tail = rolled[cb + 1]
        dst_ref[0, hh, :, cb * LANES:(cb + 1) * LANES] = jnp.where(lane < LANES - n_new, rolled[cb], tail)


def _dil_sample_kernel(q_ref, kn_ref, vn_ref, bias_ref, bn_ref, k_ref, v_ref, *out_refs, lens, n_new, write_cache):
    y_ref = out_refs[0]
    bias = bias_ref[...]
    bias_new = bn_ref[...]
    heads = range(DS_HEADS)
    scores = [(_dot(q_ref[0, hh], k_ref[0, hh].astype(BF16)), _dot(q_ref[0, hh], kn_ref[0, hh].astype(BF16)))
              for hh in heads]
    probs, lses = [], []
    for s, s_new in scores:
        s = s * B_SCALE + bias
        s_new = s_new * B_SCALE + bias_new
        m = jnp.maximum(jnp.max(s, axis=-1, keepdims=True), jnp.max(s_new, axis=-1, keepdims=True))
        e = jnp.exp(s - m)
        e_new = jnp.exp(s_new - m)
        den = jnp.sum(e, axis=-1, keepdims=True) + jnp.sum(e_new, axis=-1, keepdims=True)
        probs.append(((e / den).astype(BF16), (e_new / den).astype(BF16)))
        lses.append(m + jnp.log(den))
    outs = [_dot_nt(probs[hh][0], v_ref[0, hh].astype(BF16)) + _dot_nt(probs[hh][1], vn_ref[0, hh].astype(BF16))
            for hh in heads]
    grp = lambda a, g: a[g * SUBLANES:(g + 1) * SUBLANES]
    for hh in heads:
        lse, o = lses[hh], outs[hh]
        mm = jnp.maximum(jnp.maximum(grp(lse, 0), grp(lse, 1)), grp(lse, 2))
        ws = [jnp.exp(grp(lse, g) - mm) for g in range(N_GROUPS)]
        wsum = ws[0] + ws[1] + ws[2]
        y_ref[0, hh] = sum((ws[g] / wsum) * grp(o, g) for g in range(N_GROUPS))
    if write_cache:
        for hh in heads:
            _shift_append(k_ref[0, hh], kn_ref[0, hh], out_refs[1], hh, lens, n_new)
            _shift_append(v_ref[0, hh], vn_ref[0, hh], out_refs[2], hh, lens, n_new)


def _dil_sample(q, kn_t, vn_t, ck_t, cv_t, lens, n_new, write_cache):
    nb, hp, hd, rows = ck_t.shape
    assert all(ln % LANES == 0 for ln in lens) and sum(lens) == rows and N_GROUPS * n_new <= LANES
    bias, bias_new = _dil_sample_bias(lens, n_new)
    cache_spec = pl.BlockSpec((1, DS_HEADS, hd, rows), lambda b, j: (b, j, 0, 0))
    new_spec = pl.BlockSpec((1, DS_HEADS, hd, LANES), lambda b, j: (b, j, 0, 0))
    q_spec = pl.BlockSpec((1, DS_HEADS, DS_ROWS, hd), lambda b, j: (b, j, 0, 0))
    y_spec = pl.BlockSpec((1, DS_HEADS, SUBLANES, hd), lambda b, j: (b, j, 0, 0))
    out_specs = [y_spec]
    out_shape = [jax.ShapeDtypeStruct((nb, hp, SUBLANES, hd), F32)]
    if write_cache:
        out_specs += [cache_spec, cache_spec]
        out_shape += [jax.ShapeDtypeStruct(ck_t.shape, F32)] * 2
    return pl.pallas_call(
        functools.partial(_dil_sample_kernel, lens=lens, n_new=n_new, write_cache=write_cache),
        grid=(nb, hp // DS_HEADS),
        in_specs=[q_spec, new_spec, new_spec, _const_spec(bias.shape), _const_spec(bias_new.shape),
                  cache_spec, cache_spec],
        out_specs=out_specs,
        out_shape=out_shape,
        compiler_params=_params("parallel", "parallel"),
        name="dil_sample_update" if write_cache else "dil_sample",
    )(q, kn_t, vn_t, jnp.asarray(bias), jnp.asarray(bias_new), ck_t, cv_t)


def _mla_weights(w_in, g_q, g_kv, w_uq, w_uk, w_uv, w_o):
    w_in_p = jnp.pad(w_in, ((0, 0), (0, 4 * LANES - w_in.shape[1]))).astype(BF16)
    uq = w_uq.reshape(Q_LORA, N_HEADS_A, NOPE_A + ROPE_A)
    w_nope = uq[:, :, :NOPE_A].reshape(Q_LORA, N_HEADS_A * NOPE_A).astype(BF16)
    w_rope = jnp.pad(uq[:, :, NOPE_A:], ((0, 0), (0, 0), (0, LANES - ROPE_A)))
    w_rope = w_rope.reshape(Q_LORA, N_HEADS_A * LANES).astype(BF16)
    ukt = w_uk.transpose(1, 2, 0).reshape(N_HEADS_A // 2, 2, NOPE_A, KV_LORA)
    z = jnp.zeros_like(ukt[:, 0])
    uk_bd = jnp.concatenate([jnp.concatenate([ukt[:, 0], z], axis=2),
                             jnp.concatenate([z, ukt[:, 1]], axis=2)], axis=1).astype(BF16)
    uvt = w_uv.transpose(1, 0, 2).reshape(N_HEADS_A // 2, 2, KV_LORA, V_DIM_A)
    z = jnp.zeros_like(uvt[:, 0])
    uv_bd = jnp.concatenate([jnp.concatenate([uvt[:, 0], z], axis=2),
                             jnp.concatenate([z, uvt[:, 1]], axis=2)], axis=1).astype(BF16)
    return dict(w_in=w_in_p, g_q=g_q[None], g_kv=g_kv[None], w_nope=w_nope, w_rope=w_rope, w_uk=uk_bd,
                w_uv=uv_bd, w_o=w_o.astype(BF16))


def kernel(x_prompt, x_sample, cache_mla_latent, cache_mla_rope, cache_win_k, cache_win_v, state_conv, page_table,
           mla_w_in, mla_g_q, mla_g_kv, mla_w_uq, mla_w_uk, mla_w_uv, mla_w_o,
           dil_w_kv, dil_w_q, dil_w_o,
           ffn_w_in, ffn_conv_w, ffn_conv_b, ffn_w_out,
           ln_mix_g, ln_mix_b, ln_ffn_g, ln_ffn_b):
    batch, seq, _ = x_prompt.shape
    nb, nt, _ = x_sample.shape
    past_len = page_table.shape[1] * PAGE_SIZE
    pos_p = jnp.arange(seq, dtype=jnp.int32)
    pos_s = jnp.repeat(past_len + jnp.arange(nt, dtype=jnp.int32), nb)
    tab_a_p = _rope_tables(pos_p, ROPE_A, LANES)
    tab_a_s = _rope_tables(pos_s, ROPE_A, LANES)
    tab_b_p = _rope_tables(pos_p, ROT_DIM_B, HEAD_DIM_B)
    tab_b_s = _rope_tables(pos_s, ROT_DIM_B, HEAD_DIM_B)

    xp = x_prompt.reshape(batch * seq, D_MODEL)
    xs = x_sample.transpose(1, 0, 2).reshape(nt * nb, D_MODEL)
    ms = nt * nb
    tm_p, tm_s = 256, 128
    cache_rope_t = cache_mla_rope.transpose(0, 1, 3, 2)

    def to_batch_major(a):
        return a.reshape(nt, nb, a.shape[-1]).transpose(1, 0, 2)

    lat_p, rope_p, lat_s, rope_s, conv_p, conv_s = [], [], [], [], [], []
    for layer in range(DEPTH):
        g_mix, b_mix = ln_mix_g[layer][None], ln_mix_b[layer][None]
        if layer < N_A_LAYERS:
            wa = _mla_weights(mla_w_in[layer], mla_g_q[layer], mla_g_kv[layer], mla_w_uq[layer], mla_w_uk[layer],
                              mla_w_uv[layer], mla_w_o[layer])
            ckv, kr, kcat, q = _mla_proj(xp, wa, tab_a_p, tm_p)
            ol = _mla_flash(q, kcat, batch, seq)
            xp = _mla_out(ol, wa, xp, g_mix, b_mix, tm_p)
            lat_p.append(ckv.reshape(batch, seq, KV_LORA))
            rope_p.append(kr[:, :ROPE_A].reshape(batch, seq, ROPE_A))

            ckv_s, kr_s, _, q_s = _mla_proj(xs, wa, tab_a_s, tm_s)
            q_b = q_s.reshape(nt, N_HEADS_A, nb, MLA_QK_W).transpose(2, 0, 1, 3).reshape(nb, nt * N_HEADS_A, MLA_QK_W)
            padrows = ((0, 0), (0, SUBLANES - nt), (0, 0))
            c_new = jnp.pad(to_batch_major(ckv_s), padrows)
            r_new = jnp.pad(to_batch_major(kr_s), padrows)
            o_b = _mla_sample_attn(page_table, q_b, c_new, r_new, cache_mla_latent, cache_rope_t, layer)
            ol_s = o_b.reshape(nb, nt, N_HEADS_A * KV_LORA).transpose(1, 0, 2).reshape(ms, N_HEADS_A * KV_LORA)
            xs = _mla_out(ol_s, wa, xs, g_mix, b_mix, tm_s)
            lat_s.append(to_batch_major(ckv_s))
            rope_s.append(to_batch_major(kr_s[:, :ROPE_A]))
        else:
            if layer == N_A_LAYERS:
                w_kv = dil_w_kv.astype(BF16)
                kv_p = _proj_rope(xp, w_kv, tab_b_p, QB_W, F32, 512, lane_blocks=True)
                kv4 = kv_p.reshape(2, N_HEADS_B // 2, batch, seq, 2, HEAD_DIM_B).transpose(2, 3, 0, 1, 4, 5)
                kv4 = kv4.reshape(batch, seq, 2, N_HEADS_B, HEAD_DIM_B)
                lens_p = tuple(min(w, seq) for w in WINDOWS)
                new_win_k_prompt = jnp.concatenate(
                    [kv4[:, seq - lens_p[g]:, 0, g * HEADS_PER_GROUP:(g + 1) * HEADS_PER_GROUP]
                     for g in range(N_GROUPS)], axis=1)
                new_win_v_prompt = jnp.concatenate(
                    [kv4[:, seq - lens_p[g]:, 1, g * HEADS_PER_GROUP:(g + 1) * HEADS_PER_GROUP]
                     for g in range(N_GROUPS)], axis=1)
                kv_s = _proj_rope(xs, w_kv, tab_b_s, QB_W, F32, tm_s)
                kv_s6 = kv_s.reshape(nt, nb, 2, N_GROUPS, HEADS_PER_GROUP, HEAD_DIM_B)
                lens_s = tuple(min(w, past_len) for w in WINDOWS)

                def new_rows_t(a):
                    a = a.transpose(1, 3, 4, 2, 0).reshape(nb, HEADS_PER_GROUP, HEAD_DIM_B, N_GROUPS * nt)
                    return jnp.pad(a, ((0, 0), (0, 0), (0, 0), (0, LANES - N_GROUPS * nt)))

                kn_t, vn_t = new_rows_t(kv_s6[:, :, 0]), new_rows_t(kv_s6[:, :, 1])
                ck_t = cache_win_k.transpose(0, 2, 3, 1)
                cv_t = cache_win_v.transpose(0, 2, 3, 1)
            bi = layer - N_A_LAYERS
            w_q = dil_w_q[bi].astype(BF16)
            w_o = dil_w_o[bi].astype(BF16)
            q_p = _proj_rope(xp, w_q, tab_b_p, QB_W, F32, 512, lane_blocks=True)
            outs, lses = zip(*[_dil_attn_group(q_p, kv_p, batch, seq, g) for g in range(N_GROUPS)])
            xp = _dil_merge(outs, lses, w_o, xp, g_mix, b_mix, 512)

            q_s = _proj_rope(xs, w_q, tab_b_s, QB_W, BF16, tm_s)
            q_g = q_s.reshape(nt, nb, N_GROUPS, HEADS_PER_GROUP, HEAD_DIM_B).transpose(1, 3, 2, 0, 4)
            q_g = jnp.pad(q_g, ((0, 0), (0, 0), (0, DS_ROWS // SUBLANES - N_GROUPS), (0, SUBLANES - nt), (0, 0)))
            q_g = q_g.reshape(nb, HEADS_PER_GROUP, DS_ROWS, HEAD_DIM_B)
            write_cache = layer == N_A_LAYERS
            res = _dil_sample(q_g, kn_t, vn_t, ck_t, cv_t, lens_s, nt, write_cache)
            if write_cache:
                new_win_k_sample = res[1].transpose(0, 3, 1, 2)
                new_win_v_sample = res[2].transpose(0, 3, 1, 2)
            y_s = res[0][:, :, :nt].transpose(2, 0, 1, 3).reshape(ms, GROUP_W)
            xs = _proj_ln(y_s, w_o, xs, g_mix, b_mix, tm_s)

        wf = dict(w_in=ffn_w_in[layer].astype(BF16), conv_w=ffn_conv_w[layer], conv_b=ffn_conv_b[layer][None],
                  w_out=ffn_w_out[layer].astype(BF16))
        g_ffn, b_ffn = ln_ffn_g[layer][None], ln_ffn_b[layer][None]
        xp, st_p = _ffn_prompt(xp, wf, g_ffn, b_ffn, seq, 512)
        xs, st_s = _ffn_sample(xs, state_conv[layer].transpose(1, 0, 2), wf, g_ffn, b_ffn, nb, nt)
        conv_p.append(st_p)
        conv_s.append(st_s.transpose(1, 0, 2))

    y_prompt = xp.reshape(batch, seq, D_MODEL)
    y_sample = xs.reshape(nt, nb, D_MODEL).transpose(1, 0, 2)
    return (y_prompt, y_sample, jnp.stack(lat_p, 0), jnp.stack(rope_p, 0), new_win_k_prompt, new_win_v_prompt,
            jnp.stack(conv_p, 0), jnp.stack(lat_s, 0), jnp.stack(rope_s, 0), new_win_k_sample, new_win_v_sample,
            jnp.stack(conv_s, 0))
```

```python
import functools

import numpy as np
import jax
import jax.numpy as jnp
from jax import lax
from jax.experimental import pallas as pl
from jax.experimental.pallas import tpu as pltpu

F32 = jnp.float32
BF16 = jnp.bfloat16

D_MODEL = 1024
DEPTH = 4
PAGE_SIZE = 128
N_A_LAYERS = DEPTH // 2
N_HEADS_A = 16
NOPE_A = 64
ROPE_A = 32
V_DIM_A = 64
Q_LORA = 256
KV_LORA = 128
MLA_SCALE = (NOPE_A + ROPE_A) ** -0.5
WINDOWS = (128, 512, 2048)
DILATIONS = (1, 4, 16)
N_GROUPS = 3
HEADS_PER_GROUP = 8
HEAD_DIM_B = 64
N_HEADS_B = N_GROUPS * HEADS_PER_GROUP
ROT_DIM_B = HEAD_DIM_B // 4
B_SCALE = HEAD_DIM_B ** -0.5
BLOCK_B = 128
ROPE_THETA = 500000.0
D_FF = 2816
CONV_W = 3
ALPHA = (2.0 * DEPTH) ** 0.25
LN_EPS = 1e-5
RMS_EPS = 1e-6
NEG_INF = -1e30
SQRT_HALF = 0.7071067811865476
LOG2_E = 1.4426950408889634
MLA_Q_SCALE = MLA_SCALE * LOG2_E

LANES = 128
SUBLANES = 8
VMEM_LIMIT_BYTES = 48 * 1024 * 1024

GROUP_W = HEADS_PER_GROUP * HEAD_DIM_B
QB_W = N_HEADS_B * HEAD_DIM_B
MLA_QK_W = 2 * LANES
FFN_CHUNK = 256


def _params(*sem):
    return pltpu.CompilerParams(dimension_semantics=sem, vmem_limit_bytes=VMEM_LIMIT_BYTES)


def _const_spec(shape):
    nd = len(shape)
    return pl.BlockSpec(shape, lambda *_: (0,) * nd)


def _dot(a, b):
    return jnp.dot(a, b, preferred_element_type=F32)


def _dot_nt(a, b):
    return lax.dot_general(a, b, (((1,), (1,)), ((), ())), preferred_element_type=F32)


def _layer_norm(z, g, b):
    mu = jnp.mean(z, axis=-1, keepdims=True)
    zc = z - mu
    var = jnp.mean(zc * zc, axis=-1, keepdims=True)
    return zc * lax.rsqrt(var + LN_EPS) * g + b


def _rms_norm(a, g):
    return a * lax.rsqrt(jnp.mean(a * a, axis=-1, keepdims=True) + RMS_EPS) * g


def _rope_lanes(v, cos, sin_a, sin_b, half):
    return (v * cos + pltpu.roll(v, LANES - half, 1) * sin_a + pltpu.roll(v, half, 1) * sin_b)


def _rope_tables(pos, rot_dim, period):
    half = rot_dim // 2
    inv = ROPE_THETA ** (-jnp.arange(half, dtype=F32) * 2.0 / rot_dim)
    ang = pos.astype(F32)[:, None] * inv[None, :]
    cos, sin = jnp.cos(ang), jnp.sin(ang)
    t = pos.shape[0]
    ones = jnp.ones((t, period - rot_dim), F32)
    zr = lambda n: jnp.zeros((t, n), F32)
    c = jnp.concatenate([cos, cos, ones], axis=1)
    sa = jnp.concatenate([-sin, zr(period - half)], axis=1)
    sb = jnp.concatenate([zr(half), sin, zr(period - rot_dim)], axis=1)
    rep = LANES // period
    return tuple(jnp.tile(a, (1, rep)) for a in (c, sa, sb))


def _mla_proj_kernel(x_ref, win_ref, gq_ref, gkv_ref, cos_ref, sa_ref, sb_ref, wn_ref, wr_ref, wuk_ref,
                     ckv_ref, kr_ref, kcat_ref, q_ref, *, tm):
    xb = x_ref[...].astype(BF16)
    a = _dot(xb, win_ref[...])
    cq = _rms_norm(a[:, :Q_LORA], gq_ref[...])
    ckv = _rms_norm(a[:, Q_LORA:Q_LORA + KV_LORA], gkv_ref[...])
    cos, sa, sb = cos_ref[...], sa_ref[...], sb_ref[...]
    kr = _rope_lanes(a[:, Q_LORA + KV_LORA:], cos, sa, sb, ROPE_A // 2)
    ckv_ref[...] = ckv
    kr_ref[...] = kr
    kcat_ref[:, :LANES] = ckv.astype(BF16)
    kcat_ref[:, LANES:] = kr.astype(BF16)
    cqb = cq.astype(BF16)
    qn = _dot(cqb, wn_ref[...])
    qr = _dot(cqb, wr_ref[...])
    for p in range(N_HEADS_A // 2):
        ql2 = _dot(qn[:, p * LANES:(p + 1) * LANES].astype(BF16), wuk_ref[p])
        for hh in range(2):
            h = 2 * p + hh
            ql = (ql2[:, hh * LANES:(hh + 1) * LANES] * MLA_Q_SCALE).astype(BF16)
            rr = _rope_lanes(qr[:, h * LANES:(h + 1) * LANES], cos, sa, sb, ROPE_A // 2)
            rr = (rr * MLA_Q_SCALE).astype(BF16)
            for j in range(tm // LANES):
                q_ref[j, h, :, :LANES] = ql[j * LANES:(j + 1) * LANES]
                q_ref[j, h, :, LANES:] = rr[j * LANES:(j + 1) * LANES]


def _mla_proj(x, w, tables, tm):
    m = x.shape[0]
    cos, sa, sb = tables
    nper = cos.shape[0] // tm
    tab_spec = pl.BlockSpec((tm, LANES), lambda i: (i % nper, 0))
    row = lambda n: pl.BlockSpec((tm, n), lambda i: (i, 0))
    return pl.pallas_call(
        functools.partial(_mla_proj_kernel, tm=tm),
        grid=(m // tm,),
        in_specs=[row(D_MODEL), _const_spec(w["w_in"].shape), _const_spec((1, Q_LORA)), _const_spec((1, KV_LORA)),
                  tab_spec, tab_spec, tab_spec, _const_spec(w["w_nope"].shape), _const_spec(w["w_rope"].shape),
                  _const_spec(w["w_uk"].shape)],
        out_specs=[row(KV_LORA), row(LANES), row(MLA_QK_W),
                   pl.BlockSpec((tm // LANES, N_HEADS_A, LANES, MLA_QK_W), lambda i: (i, 0, 0, 0))],
        out_shape=[jax.ShapeDtypeStruct((m, KV_LORA), F32), jax.ShapeDtypeStruct((m, LANES), F32),
                   jax.ShapeDtypeStruct((m, MLA_QK_W), BF16),
                   jax.ShapeDtypeStruct((m // LANES, N_HEADS_A, LANES, MLA_QK_W), BF16)],
        compiler_params=_params("parallel"),
        name="mla_proj",
    )(x, w["w_in"], w["g_q"], w["g_kv"], cos, sa, sb, w["w_nope"], w["w_rope"], w["w_uk"])


MLA_TQ = 256
MLA_TK = 512
MLA_HEADS_PER_DOT = 8
MLA_SUB = MLA_TQ // LANES


def _mla_flash_kernel(qi_ref, kj_ref, q_ref, k_ref, o_ref, m_ref, acc_ref):
    t = pl.program_id(1)
    qi = qi_ref[t]
    kj = kj_ref[t]
    last = (qi * MLA_TQ + MLA_TQ - 1) // MLA_TK

    @pl.when(kj == 0)
    def _():
        m_ref[...] = jnp.full(m_ref.shape, NEG_INF, F32)
        acc_ref[...] = jnp.zeros(acc_ref.shape, F32)

    k = k_ref[...]
    v_ones = jnp.concatenate([k[:, :KV_LORA], jnp.ones((MLA_TK, LANES), BF16)], axis=1)

    nrow = MLA_HEADS_PER_DOT * LANES
    groups = N_HEADS_A // MLA_HEADS_PER_DOT

    def step(masked):
        for sb in range(MLA_SUB):
            if masked:
                q_pos = qi * MLA_TQ + sb * LANES + (lax.broadcasted_iota(jnp.int32, (nrow, MLA_TK), 0) & (LANES - 1))
                k_pos = kj * MLA_TK + lax.broadcasted_iota(jnp.int32, (nrow, MLA_TK), 1)
                visible = k_pos <= q_pos
            for hg in range(groups):
                rows = pl.ds((sb * groups + hg) * nrow, nrow)
                q = q_ref[sb, hg * MLA_HEADS_PER_DOT:(hg + 1) * MLA_HEADS_PER_DOT].reshape(nrow, MLA_QK_W)
                s = _dot_nt(q, k)
                if masked:
                    s = jnp.where(visible, s, NEG_INF)
                m_prev = m_ref[rows, :]
                m_new = jnp.maximum(m_prev, jnp.max(s, axis=-1, keepdims=True))
                alpha = jnp.exp2(m_prev - m_new)
                p = jnp.exp2(s - jnp.concatenate([m_new] * (MLA_TK // LANES), axis=1))
                pv = _dot(p.astype(BF16), v_ones)
                acc_ref[rows, :] = jnp.concatenate([alpha, alpha], axis=1) * acc_ref[rows, :] + pv
                m_ref[rows, :] = m_new

    @pl.when(kj != last)
    def _():
        step(False)

    @pl.when(kj == last)
    def _():
        step(True)
        for sb in range(MLA_SUB):
            for h in range(N_HEADS_A):
                a = acc_ref[pl.ds((sb * N_HEADS_A + h) * LANES, LANES), :]
                o_ref[sb * LANES:(sb + 1) * LANES, h * KV_LORA:(h + 1) * KV_LORA] = (
                    a[:, :KV_LORA] / a[:, KV_LORA:]).astype(BF16)


def _mla_flash(q, kcat, batch, seq):
    nq = seq // MLA_TQ
    nk = seq // MLA_TK
    pairs = [(i, j) for i in range(nq) for j in range((i * MLA_TQ + MLA_TQ - 1) // MLA_TK + 1)]
    qi = jnp.asarray(np.array([p[0] for p in pairs], np.int32))
    kj = jnp.asarray(np.array([p[1] for p in pairs], np.int32))
    rows = N_HEADS_A * MLA_TQ
    return pl.pallas_call(
        _mla_flash_kernel,
        grid_spec=pltpu.PrefetchScalarGridSpec(
            num_scalar_prefetch=2,
            grid=(batch, len(pairs)),
            in_specs=[pl.BlockSpec((MLA_SUB, N_HEADS_A, LANES, MLA_QK_W), lambda b, t, qi, kj: (b * nq + qi[t], 0, 0, 0)),
                      pl.BlockSpec((MLA_TK, MLA_QK_W), lambda b, t, qi, kj: (b * nk + kj[t], 0))],
            out_specs=pl.BlockSpec((MLA_TQ, N_HEADS_A * KV_LORA), lambda b, t, qi, kj: (b * nq + qi[t], 0)),
            scratch_shapes=[pltpu.VMEM((rows, LANES), F32), pltpu.VMEM((rows, KV_LORA + LANES), F32)],
        ),
        out_shape=jax.ShapeDtypeStruct((batch * seq, N_HEADS_A * KV_LORA), BF16),
        compiler_params=_params("parallel", "arbitrary"),
        name="mla_flash",
    )(qi, kj, q, kcat)


MLA_SAMPLE_SEQS = 2


def _mla_sample_kernel(pt_ref, q_ref, cn_ref, rn_ref, lat_hbm, rope_hbm, o_ref, lat_buf, rope_buf, sem,
                       *, layer, n_pages, n_new):
    step = pl.program_id(0)
    n_steps = pl.num_programs(0)
    slot = step % 2
    seqs = range(MLA_SAMPLE_SEQS)

    def lat_copy(st, sl, u, j):
        page = pt_ref[st * MLA_SAMPLE_SEQS + u, j]
        return pltpu.make_async_copy(lat_hbm.at[layer, page], lat_buf.at[sl, u, j], sem.at[sl, 0])

    def rope_copy(st, sl, u, j):
        page = pt_ref[st * MLA_SAMPLE_SEQS + u, j]
        return pltpu.make_async_copy(rope_hbm.at[layer, page],
                                     rope_buf.at[sl, u, :, pl.ds(j * PAGE_SIZE, PAGE_SIZE)], sem.at[sl, 1])

    def start_all(st, sl):
        for u in seqs:
            for j in range(n_pages):
                lat_copy(st, sl, u, j).start()
                rope_copy(st, sl, u, j).start()

    @pl.when(step == 0)
    def _():
        start_all(0, 0)

    @pl.when(step + 1 < n_steps)
    def _():
        start_all(step + 1, 1 - slot)

    for u in seqs:
        for j in range(n_pages):
            lat_copy(step, slot, u, j).wait()
            rope_copy(step, slot, u, j).wait()

    n_keys = n_pages * PAGE_SIZE
    lats, cns, scores = [], [], []
    for u in seqs:
        lat = lat_buf[slot, u].reshape(n_keys, KV_LORA).astype(BF16)
        rope_t = rope_buf[slot, u].astype(BF16)
        q = q_ref[u]
        q_lat = q[:, :KV_LORA]
        cn = cn_ref[u].astype(BF16)
        rn = rn_ref[u].astype(BF16)
        s = _dot_nt(q_lat, lat) + _dot(q[:, KV_LORA:KV_LORA + ROPE_A], rope_t)
        s_new = _dot_nt(q_lat, cn) + _dot_nt(q[:, KV_LORA:], rn)
        lats.append(lat)
        cns.append(cn)
        scores.append((s, s_new))
    probs = []
    for s, s_new in scores:
        t_q = lax.broadcasted_iota(jnp.int32, s_new.shape, 0) // N_HEADS_A
        j_k = lax.broadcasted_iota(jnp.int32, s_new.shape, 1)
        s_new = jnp.where(j_k <= t_q, s_new, NEG_INF)
        m = jnp.maximum(jnp.max(s, axis=-1, keepdims=True), jnp.max(s_new, axis=-1, keepdims=True))
        e = jnp.exp2(s - m)
        e_new = jnp.exp2(s_new - m)
        den = jnp.sum(e, axis=-1, keepdims=True) + jnp.sum(e_new, axis=-1, keepdims=True)
        probs.append(((e / den).astype(BF16), (e_new / den).astype(BF16)))
    for u in seqs:
        o_ref[u] = (_dot(probs[u][0], lats[u]) + _dot(probs[u][1], cns[u])).astype(BF16)


def _mla_sample_attn(page_table, q, c_new, r_new, cache_lat, cache_rope_t, layer):
    nb, n_pages = page_table.shape
    rows = q.shape[1]
    n_new = rows // N_HEADS_A
    ns = MLA_SAMPLE_SEQS
    assert nb % ns == 0
    return pl.pallas_call(
        functools.partial(_mla_sample_kernel, layer=layer, n_pages=n_pages, n_new=n_new),
        grid_spec=pltpu.PrefetchScalarGridSpec(
            num_scalar_prefetch=1,
            grid=(nb // ns,),
            in_specs=[pl.BlockSpec((ns, rows, MLA_QK_W), lambda b, pt: (b, 0, 0)),
                      pl.BlockSpec((ns, SUBLANES, KV_LORA), lambda b, pt: (b, 0, 0)),
                      pl.BlockSpec((ns, SUBLANES, LANES), lambda b, pt: (b, 0, 0)),
                      pl.BlockSpec(memory_space=pl.ANY),
                      pl.BlockSpec(memory_space=pl.ANY)],
            out_specs=pl.BlockSpec((ns, rows, KV_LORA), lambda b, pt: (b, 0, 0)),
            scratch_shapes=[pltpu.VMEM((2, ns, n_pages, PAGE_SIZE, KV_LORA), F32),
                            pltpu.VMEM((2, ns, ROPE_A, n_pages * PAGE_SIZE), F32),
                            pltpu.SemaphoreType.DMA((2, 2))],
        ),
        out_shape=jax.ShapeDtypeStruct((nb, rows, KV_LORA), BF16),
        compiler_params=_params("arbitrary"),
        name="mla_sample_attn",
    )(page_table, q, c_new, r_new, cache_lat, cache_rope_t)


def _mla_out_kernel(ol_ref, wuv_ref, wo_ref, x_ref, g_ref, b_ref, y_ref):
    parts = [_dot(ol_ref[:, p * 2 * KV_LORA:(p + 1) * 2 * KV_LORA], wuv_ref[p]) for p in range(N_HEADS_A // 2)]
    o = jnp.concatenate(parts, axis=1).astype(BF16)
    mix = _dot(o, wo_ref[...])
    y_ref[...] = _layer_norm(ALPHA * x_ref[...] + mix, g_ref[...], b_ref[...])


def _mla_out(ol, w, x, g, b, tm):
    m = x.shape[0]
    row = lambda n: pl.BlockSpec((tm, n), lambda i: (i, 0))
    return pl.pallas_call(
        _mla_out_kernel,
        grid=(m // tm,),
        in_specs=[row(N_HEADS_A * KV_LORA), _const_spec(w["w_uv"].shape), _const_spec(w["w_o"].shape),
                  row(D_MODEL), _const_spec((1, D_MODEL)), _const_spec((1, D_MODEL))],
        out_specs=row(D_MODEL),
        out_shape=jax.ShapeDtypeStruct((m, D_MODEL), F32),
        compiler_params=_params("parallel"),
        name="mla_out",
    )(ol, w["w_uv"], w["w_o"], x, g, b)


def _proj_ln_kernel(a_ref, wo_ref, x_ref, g_ref, b_ref, y_ref):
    mix = _dot(a_ref[...].astype(BF16), wo_ref[...])
    y_ref[...] = _layer_norm(ALPHA * x_ref[...] + mix, g_ref[...], b_ref[...])


def _proj_ln(a, wo, x, g, b, tm):
    m = x.shape[0]
    row = lambda n: pl.BlockSpec((tm, n), lambda i: (i, 0))
    return pl.pallas_call(
        _proj_ln_kernel,
        grid=(m // tm,),
        in_specs=[row(a.shape[1]), _const_spec(wo.shape), row(D_MODEL), _const_spec((1, D_MODEL)),
                  _const_spec((1, D_MODEL))],
        out_specs=row(D_MODEL),
        out_shape=jax.ShapeDtypeStruct((m, D_MODEL), F32),
        compiler_params=_params("parallel"),
        name="proj_ln",
    )(a, wo, x, g, b)


def _dil_merge_kernel(o0, o1, o2, l0, l1, l2, wo_ref, x_ref, g_ref, b_ref, y_ref):
    wide = lambda ref: jnp.concatenate([ref[j] for j in range(ref.shape[0])], axis=1)
    la, lb, lc = wide(l0), wide(l1), wide(l2)
    m = jnp.maximum(jnp.maximum(la, lb), lc)
    ea, eb, ec = jnp.exp(la - m), jnp.exp(lb - m), jnp.exp(lc - m)
    den = ea + eb + ec
    y = (ea / den) * wide(o0) + (eb / den) * wide(o1) + (ec / den) * wide(o2)
    mix = _dot(y.astype(BF16), wo_ref[...])
    y_ref[...] = _layer_norm(ALPHA * x_ref[...] + mix, g_ref[...], b_ref[...])


def _dil_merge(outs, lses, wo, x, g, b, tm):
    m = x.shape[0]
    row = lambda n: pl.BlockSpec((tm, n), lambda i: (i, 0))
    return pl.pallas_call(
        _dil_merge_kernel,
        grid=(m // tm,),
        in_specs=[pl.BlockSpec((outs[0].shape[0], tm, LANES), lambda i: (0, i, 0))] * 6
        + [_const_spec(wo.shape), row(D_MODEL), _const_spec((1, D_MODEL)), _const_spec((1, D_MODEL))],
        out_specs=row(D_MODEL),
        out_shape=jax.ShapeDtypeStruct((m, D_MODEL), F32),
        compiler_params=_params("parallel"),
        name="dil_merge",
    )(*outs, *lses, wo, x, g, b)


def _gelu_gate(acc, val):
    return 0.5 * acc * (1.0 + lax.erf(acc * SQRT_HALF)) * val


def _ffn_prompt_kernel(x_ref, xh_ref, win_ref, cw_ref, cb_ref, wout_ref, g_ref, b_ref, y_ref, st_ref,
                       *, tm, tiles_per_seq):
    i = pl.program_id(0)
    x = x_ref[...]
    xb = x.astype(BF16)
    xhb = xh_ref[...].astype(BF16)
    keep = ((i % tiles_per_seq) != 0).astype(F32)
    acc = jnp.zeros((tm, D_MODEL), F32)
    n_chunks = D_FF // FFN_CHUNK

    def up_proj(c):
        lo = c * FFN_CHUNK
        wg = win_ref[:, lo:lo + FFN_CHUNK]
        return (_dot(xb, wg), _dot(xb, win_ref[:, D_FF + lo:D_FF + lo + FFN_CHUNK]), _dot(xhb, wg))

    nxt = up_proj(0)
    for c in range(n_chunks):
        lo = c * FFN_CHUNK
        gate, val, gh = nxt
        if c + 1 < n_chunks:
            nxt = up_proj(c + 1)
        ext = jnp.concatenate([gh * keep, gate], axis=0)
        g1 = ext[SUBLANES - 1:SUBLANES - 1 + tm]
        g2 = ext[SUBLANES - 2:SUBLANES - 2 + tm]
        a = cb_ref[:, lo:lo + FFN_CHUNK] + cw_ref[0:1, lo:lo + FFN_CHUNK] * g2
        a = a + cw_ref[1:2, lo:lo + FFN_CHUNK] * g1
        a = a + cw_ref[2:3, lo:lo + FFN_CHUNK] * gate
        h = _gelu_gate(a, val).astype(BF16)
        acc = acc + _dot(h, wout_ref[lo:lo + FFN_CHUNK, :])
        st_ref[0, :, lo:lo + FFN_CHUNK] = gate[tm - SUBLANES:]
    y_ref[...] = _layer_norm(ALPHA * x + acc, g_ref[...], b_ref[...])


def _ffn_prompt(x, w, g, b, seq, tm):
    m = x.shape[0]
    tiles_per_seq = seq // tm
    hb = tm // SUBLANES
    row = lambda n: pl.BlockSpec((tm, n), lambda i: (i, 0))
    y, st = pl.pallas_call(
        functools.partial(_ffn_prompt_kernel, tm=tm, tiles_per_seq=tiles_per_seq),
        grid=(m // tm,),
        in_specs=[row(D_MODEL),
                  pl.BlockSpec((SUBLANES, D_MODEL), lambda i: (jnp.maximum(i * hb - 1, 0), 0)),
                  _const_spec(w["w_in"].shape), _const_spec((CONV_W, D_FF)), _const_spec((1, D_FF)),
                  _const_spec(w["w_out"].shape), _const_spec((1, D_MODEL)), _const_spec((1, D_MODEL))],
        out_specs=[row(D_MODEL), pl.BlockSpec((1, SUBLANES, D_FF), lambda i: (i // tiles_per_seq, 0, 0))],
        out_shape=[jax.ShapeDtypeStruct((m, D_MODEL), F32),
                   jax.ShapeDtypeStruct((m // seq, SUBLANES, D_FF), F32)],
        compiler_params=_params("arbitrary"),
        name="ffn_prompt",
    )(x, x, w["w_in"], w["conv_w"], w["conv_b"], w["w_out"], g, b)
    return y, st[:, SUBLANES - (CONV_W - 1):]


def _ffn_sample_kernel(x_ref, s_ref, win_ref, cw_ref, cb_ref, wout_ref, g_ref, b_ref, y_ref, st_ref, *, nb, nt):
    x = x_ref[...]
    xb = x.astype(BF16)
    acc = jnp.zeros((nt * nb, D_MODEL), F32)
    for c in range(D_FF // FFN_CHUNK):
        lo = c * FFN_CHUNK
        gate = _dot(xb, win_ref[:, lo:lo + FFN_CHUNK])
        val = _dot(xb, win_ref[:, D_FF + lo:D_FF + lo + FFN_CHUNK])
        s0 = s_ref[0, :, lo:lo + FFN_CHUNK]
        s1 = s_ref[1, :, lo:lo + FFN_CHUNK]
        g1 = jnp.concatenate([s1, gate[:(nt - 1) * nb]], axis=0)
        g2 = jnp.concatenate([s0, s1, gate[:(nt - 2) * nb]], axis=0)
        a = cb_ref[:, lo:lo + FFN_CHUNK] + cw_ref[0:1, lo:lo + FFN_CHUNK] * g2
        a = a + cw_ref[1:2, lo:lo + FFN_CHUNK] * g1
        a = a + cw_ref[2:3, lo:lo + FFN_CHUNK] * gate
        h = _gelu_gate(a, val).astype(BF16)
        acc = acc + _dot(h, wout_ref[lo:lo + FFN_CHUNK, :])
        st_ref[0, :, lo:lo + FFN_CHUNK] = gate[(nt - 2) * nb:(nt - 1) * nb]
        st_ref[1, :, lo:lo + FFN_CHUNK] = gate[(nt - 1) * nb:]
    y_ref[...] = _layer_norm(ALPHA * x + acc, g_ref[...], b_ref[...])


def _ffn_sample(x, state, w, g, b, nb, nt):
    m = x.shape[0]
    return pl.pallas_call(
        functools.partial(_ffn_sample_kernel, nb=nb, nt=nt),
        out_shape=[jax.ShapeDtypeStruct((m, D_MODEL), F32), jax.ShapeDtypeStruct((CONV_W - 1, nb, D_FF), F32)],
        compiler_params=pltpu.CompilerParams(vmem_limit_bytes=VMEM_LIMIT_BYTES),
        name="ffn_sample",
    )(x, state, w["w_in"], w["conv_w"], w["conv_b"], w["w_out"], g, b)


PROJ_CHUNK = 512


def _proj_rope_kernel(x_ref, w_ref, cos_ref, sa_ref, sb_ref, o_ref, *win_refs, n, rope_cols, lane_blocks, tm,
                      windows, tiles_per_seq):
    xb = x_ref[...].astype(BF16)
    cos, sa, sb = cos_ref[...], sa_ref[...], sb_ref[...]
    is_last_tile = (pl.program_id(0) % tiles_per_seq) == tiles_per_seq - 1 if windows else None
    for c in range(n // PROJ_CHUNK):
        lo = c * PROJ_CHUNK
        y = _dot(xb, w_ref[:, lo:lo + PROJ_CHUNK])
        for k in range(PROJ_CHUNK // LANES):
            slab = lo // LANES + k
            yk = y[:, k * LANES:(k + 1) * LANES]
            if lo < rope_cols:
                yk = _rope_lanes(yk, cos, sa, sb, ROT_DIM_B // 2)
            if lane_blocks:
                o_ref[slab] = yk.astype(o_ref.dtype)
            else:
                o_ref[:, lo + k * LANES:lo + (k + 1) * LANES] = yk.astype(o_ref.dtype)
            if windows:
                kv_sel, rest = divmod(slab, N_GROUPS * GROUP_SLABS)
                g, j = divmod(rest, GROUP_SLABS)
                dst = win_refs[kv_sel * N_GROUPS + g]
                width = min(windows[g], tm)

                def emit(dst=dst, j=j, width=width, yk=yk):
                    t = jnp.transpose(yk[tm - width:])
                    dst[0, 2 * j:2 * j + 2] = t.reshape(2, HEAD_DIM_B, width)

                if windows[g] >= tiles_per_seq * tm:
                    emit()
                else:
                    pl.when(is_last_tile)(emit)


def _proj_rope(x, w, tables, rope_cols, out_dtype, tm, lane_blocks=False, windows=None, seq=None):
    m = x.shape[0]
    n = w.shape[1]
    cos, sa, sb = tables
    nper = cos.shape[0] // tm
    tab_spec = pl.BlockSpec((tm, LANES), lambda i: (i % nper, 0))
    if lane_blocks:
        out_specs = [pl.BlockSpec((n // LANES, tm, LANES), lambda i: (0, i, 0))]
        out_shape = [jax.ShapeDtypeStruct((n // LANES, m, LANES), out_dtype)]
    else:
        out_specs = [pl.BlockSpec((tm, n), lambda i: (i, 0))]
        out_shape = [jax.ShapeDtypeStruct((m, n), out_dtype)]
    tiles_per_seq = None
    if windows:
        tiles_per_seq = seq // tm
        assert all(wl == seq or wl <= tm for wl in windows) and n == 2 * QB_W
        for _ in range(2):
            for wl in windows:
                if wl == seq:
                    spec = pl.BlockSpec((1, HEADS_PER_GROUP, HEAD_DIM_B, tm),
                                        lambda i: (i // tiles_per_seq, 0, 0, i % tiles_per_seq))
                else:
                    spec = pl.BlockSpec((1, HEADS_PER_GROUP, HEAD_DIM_B, wl), lambda i: (i // tiles_per_seq, 0, 0, 0))
                out_specs.append(spec)
                out_shape.append(jax.ShapeDtypeStruct((m // seq, HEADS_PER_GROUP, HEAD_DIM_B, wl), F32))
    res = pl.pallas_call(
        functools.partial(_proj_rope_kernel, n=n, rope_cols=rope_cols, lane_blocks=lane_blocks, tm=tm,
                          windows=windows, tiles_per_seq=tiles_per_seq),
        grid=(m // tm,),
        in_specs=[pl.BlockSpec((tm, D_MODEL), lambda i: (i, 0)), _const_spec(w.shape), tab_spec, tab_spec, tab_spec],
        out_specs=out_specs,
        out_shape=out_shape,
        compiler_params=_params("arbitrary" if windows else "parallel"),
        name="proj_rope",
    )(x, w, cos, sa, sb)
    return res if windows else res[0]


GROUP_SLABS = GROUP_W // LANES


def _dil_attn_kernel(q_ref, k_ref, v_ref, o_ref, l_ref, *, d, span):
    seq = q_ref.shape[2]
    nblk = seq // (BLOCK_B * d)
    has_prev = nblk > 1
    nk = 2 * BLOCK_B if has_prev else BLOCK_B
    a = lax.broadcasted_iota(jnp.int32, (BLOCK_B, nk), 0)
    c = lax.broadcasted_iota(jnp.int32, (BLOCK_B, nk), 1)
    diff = a + (nk - BLOCK_B) - c
    in_band = (diff >= 0) & (diff <= span)

    r_low_count = min(d, SUBLANES)

    def phase_rows(ref, n, r):
        r_hi, r_lo = r
        start = pl.multiple_of(n * (BLOCK_B * d) + r_hi * SUBLANES, SUBLANES) + r_lo
        return jnp.concatenate([ref[j, 0, pl.ds(start, BLOCK_B, stride=d), :] for j in range(GROUP_SLABS)], axis=1)

    def body(it, carry, r_lo):
        r = (it // nblk, r_lo)
        n = it % nblk
        q = phase_rows(q_ref, n, r).astype(BF16)
        if has_prev:
            n_prev = jnp.maximum(n - 1, 0)
            k = jnp.concatenate([phase_rows(k_ref, n_prev, r), phase_rows(k_ref, n, r)], axis=0).astype(BF16)
            v = jnp.concatenate([phase_rows(v_ref, n_prev, r), phase_rows(v_ref, n, r)], axis=0).astype(BF16)
            mask = in_band & ((n - 1) * BLOCK_B + c >= 0)
        else:
            k = phase_rows(k_ref, n, r).astype(BF16)
            v = phase_rows(v_ref, n, r).astype(BF16)
            mask = in_band
        head = lambda a, h: a[:, h * HEAD_DIM_B:(h + 1) * HEAD_DIM_B]
        scores = [_dot_nt(head(q, h), head(k, h)) for h in range(HEADS_PER_GROUP)]
        probs, lses = [], []
        for s in scores:
            s = jnp.where(mask, s * B_SCALE, NEG_INF)
            m = jnp.max(s, axis=-1, keepdims=True)
            e = jnp.exp(s - m)
            den = jnp.sum(e, axis=-1, keepdims=True)
            probs.append((e / den).astype(BF16))
            lses.append(jnp.broadcast_to(m + jnp.log(den), (BLOCK_B, HEAD_DIM_B)))
        outs = [_dot(probs[h], head(v, h)) for h in range(HEADS_PER_GROUP)]
        start = pl.multiple_of(n * (BLOCK_B * d) + r[0] * SUBLANES, SUBLANES) + r_lo
        rows = pl.ds(start, BLOCK_B, stride=d)
        for j in range(GROUP_SLABS):
            o_ref[j, 0, rows, :] = jnp.concatenate(outs[2 * j:2 * j + 2], axis=1)
            l_ref[j, 0, rows, :] = jnp.concatenate(lses[2 * j:2 * j + 2], axis=1)
        return carry

    for r_lo in range(r_low_count):
        trips = (d // r_low_count) * nblk
        lax.fori_loop(0, trips, functools.partial(body, r_lo=r_lo), 0, unroll=2 if trips > 2 else 1)


def _dil_attn_group(q, kv, batch, seq, g):
    d = DILATIONS[g]
    span = WINDOWS[g] // d
    nq = QB_W // GROUP_W
    blk = (GROUP_SLABS, 1, seq, LANES)
    col = lambda j: pl.BlockSpec(blk, lambda b: (j, b, 0, 0))
    shape = jax.ShapeDtypeStruct((GROUP_SLABS, batch, seq, LANES), F32)
    q4 = q.reshape(q.shape[0], batch, seq, LANES)
    kv4 = kv.reshape(kv.shape[0], batch, seq, LANES)
    o, l = pl.pallas_call(
        functools.partial(_dil_attn_kernel, d=d, span=span),
        grid=(batch,),
        in_specs=[col(g), col(g), col(nq + g)],
        out_specs=[col(0), col(0)],
        out_shape=[shape, shape],
        compiler_params=_params("parallel"),
        name=f"dil_attn_g{g}",
    )(q4, kv4, kv4)
    return o.reshape(GROUP_SLABS, batch * seq, LANES), l.reshape(GROUP_SLABS, batch * seq, LANES)


DS_HEADS = 4
DS_ROWS = 4 * SUBLANES


def _dil_sample_bias(lens, n_new):
    bias = np.full((DS_ROWS, sum(lens)), NEG_INF, np.float32)
    bias_new = np.full((DS_ROWS, LANES), NEG_INF, np.float32)
    off = 0
    for g, ln in enumerate(lens):
        d = DILATIONS[g]
        span = WINDOWS[g] // d
        for t in range(n_new):
            rel = ln + t - np.arange(ln)
            ok = (rel % d == 0) & (rel // d <= span)
            bias[g * SUBLANES + t, off:off + ln] = np.where(ok, 0.0, NEG_INF)
            for t2 in range(t + 1):
                if (t - t2) % d == 0 and (t - t2) // d <= span:
                    bias_new[g * SUBLANES + t, n_new * g + t2] = 0.0
        off += ln
    return bias, bias_new


def _shift_append(src, new_t, dst_ref, hh, lens, n_new):
    nblk = src.shape[1] // LANES
    ends = {}
    off = 0
    for g, ln in enumerate(lens):
        off += ln
        ends[off // LANES - 1] = g
    lane = lax.broadcasted_iota(jnp.int32, (src.shape[0], LANES), 1)
    rolled = [pltpu.roll(src[:, cb * LANES:(cb + 1) * LANES], LANES - n_new, 1) for cb in range(nblk)]
    for cb in range(nblk):
        if cb in ends:
            tail = pltpu.roll(new_t, LANES - n_new - n_new * ends[cb], 1)
        else:
            tail = rolled[cb + 1]
        dst_ref[0, hh, :, cb * LANES:(cb + 1) * LANES] = jnp.where(lane < LANES - n_new, rolled[cb], tail)


def _dil_sample_kernel(q_ref, kn_ref, vn_ref, bias_ref, bn_ref, k_ref, v_ref, *out_refs, lens, n_new, write_cache):
    y_ref = out_refs[0]
    bias = bias_ref[...]
    bias_new = bn_ref[...]
    heads = range(DS_HEADS)
    scores = [(_dot(q_ref[0, hh], k_ref[0, hh].astype(BF16)), _dot(q_ref[0, hh], kn_ref[0, hh].astype(BF16)))
              for hh in heads]
    probs, lses = [], []
    for s, s_new in scores:
        s = s * B_SCALE + bias
        s_new = s_new * B_SCALE + bias_new
        m = jnp.maximum(jnp.max(s, axis=-1, keepdims=True), jnp.max(s_new, axis=-1, keepdims=True))
        e = jnp.exp(s - m)
        e_new = jnp.exp(s_new - m)
        den = jnp.sum(e, axis=-1, keepdims=True) + jnp.sum(e_new, axis=-1, keepdims=True)
        probs.append(((e / den).astype(BF16), (e_new / den).astype(BF16)))
        lses.append(m + jnp.log(den))
    outs = [_dot_nt(probs[hh][0], v_ref[0, hh].astype(BF16)) + _dot_nt(probs[hh][1], vn_ref[0, hh].astype(BF16))
            for hh in heads]
    grp = lambda a, g: a[g * SUBLANES:(g + 1) * SUBLANES]
    for hh in heads:
        lse, o = lses[hh], outs[hh]
        mm = jnp.maximum(jnp.maximum(grp(lse, 0), grp(lse, 1)), grp(lse, 2))
        ws = [jnp.exp(grp(lse, g) - mm) for g in range(N_GROUPS)]
        wsum = ws[0] + ws[1] + ws[2]
        y_ref[0, hh] = sum((ws[g] / wsum) * grp(o, g) for g in range(N_GROUPS))
    if write_cache:
        for hh in heads:
            _shift_append(k_ref[0, hh], kn_ref[0, hh], out_refs[1], hh, lens, n_new)
            _shift_append(v_ref[0, hh], vn_ref[0, hh], out_refs[2], hh, lens, n_new)


def _dil_sample(q, kn_t, vn_t, ck_t, cv_t, lens, n_new, write_cache):
    nb, hp, hd, rows = ck_t.shape
    assert all(ln % LANES == 0 for ln in lens) and sum(lens) == rows and N_GROUPS * n_new <= LANES
    bias, bias_new = _dil_sample_bias(lens, n_new)
    cache_spec = pl.BlockSpec((1, DS_HEADS, hd, rows), lambda b, j: (b, j, 0, 0))
    new_spec = pl.BlockSpec((1, DS_HEADS, hd, LANES), lambda b, j: (b, j, 0, 0))
    q_spec = pl.BlockSpec((1, DS_HEADS, DS_ROWS, hd), lambda b, j: (b, j, 0, 0))
    y_spec = pl.BlockSpec((1, DS_HEADS, SUBLANES, hd), lambda b, j: (b, j, 0, 0))
    out_specs = [y_spec]
    out_shape = [jax.ShapeDtypeStruct((nb, hp, SUBLANES, hd), F32)]
    if write_cache:
        out_specs += [cache_spec, cache_spec]
        out_shape += [jax.ShapeDtypeStruct(ck_t.shape, F32)] * 2
    return pl.pallas_call(
        functools.partial(_dil_sample_kernel, lens=lens, n_new=n_new, write_cache=write_cache),
        grid=(nb, hp // DS_HEADS),
        in_specs=[q_spec, new_spec, new_spec, _const_spec(bias.shape), _const_spec(bias_new.shape),
                  cache_spec, cache_spec],
        out_specs=out_specs,
        out_shape=out_shape,
        compiler_params=_params("parallel", "parallel"),
        name="dil_sample_update" if write_cache else "dil_sample",
    )(q, kn_t, vn_t, jnp.asarray(bias), jnp.asarray(bias_new), ck_t, cv_t)


def _mla_weights(w_in, g_q, g_kv, w_uq, w_uk, w_uv, w_o):
    w_in_p = jnp.pad(w_in, ((0, 0), (0, 4 * LANES - w_in.shape[1]))).astype(BF16)
    uq = w_uq.reshape(Q_LORA, N_HEADS_A, NOPE_A + ROPE_A)
    w_nope = uq[:, :, :NOPE_A].reshape(Q_LORA, N_HEADS_A * NOPE_A).astype(BF16)
    w_rope = jnp.pad(uq[:, :, NOPE_A:], ((0, 0), (0, 0), (0, LANES - ROPE_A)))
    w_rope = w_rope.reshape(Q_LORA, N_HEADS_A * LANES).astype(BF16)
    ukt = w_uk.transpose(1, 2, 0).reshape(N_HEADS_A // 2, 2, NOPE_A, KV_LORA)
    z = jnp.zeros_like(ukt[:, 0])
    uk_bd = jnp.concatenate([jnp.concatenate([ukt[:, 0], z], axis=2),
                             jnp.concatenate([z, ukt[:, 1]], axis=2)], axis=1).astype(BF16)
    uvt = w_uv.transpose(1, 0, 2).reshape(N_HEADS_A // 2, 2, KV_LORA, V_DIM_A)
    z = jnp.zeros_like(uvt[:, 0])
    uv_bd = jnp.concatenate([jnp.concatenate([uvt[:, 0], z], axis=2),
                             jnp.concatenate([z, uvt[:, 1]], axis=2)], axis=1).astype(BF16)
    return dict(w_in=w_in_p, g_q=g_q[None], g_kv=g_kv[None], w_nope=w_nope, w_rope=w_rope, w_uk=uk_bd,
                w_uv=uv_bd, w_o=w_o.astype(BF16))


def kernel(x_prompt, x_sample, cache_mla_latent, cache_mla_rope, cache_win_k, cache_win_v, state_conv, page_table,
           mla_w_in, mla_g_q, mla_g_kv, mla_w_uq, mla_w_uk, mla_w_uv, mla_w_o,
           dil_w_kv, dil_w_q, dil_w_o,
           ffn_w_in, ffn_conv_w, ffn_conv_b, ffn_w_out,
           ln_mix_g, ln_mix_b, ln_ffn_g, ln_ffn_b):
    batch, seq, _ = x_prompt.shape
    nb, nt, _ = x_sample.shape
    past_len = page_table.shape[1] * PAGE_SIZE
    pos_p = jnp.arange(seq, dtype=jnp.int32)
    pos_s = jnp.repeat(past_len + jnp.arange(nt, dtype=jnp.int32), nb)
    tab_a_p = _rope_tables(pos_p, ROPE_A, LANES)
    tab_a_s = _rope_tables(pos_s, ROPE_A, LANES)
    tab_b_p = _rope_tables(pos_p, ROT_DIM_B, HEAD_DIM_B)
    tab_b_s = _rope_tables(pos_s, ROT_DIM_B, HEAD_DIM_B)

    xp = x_prompt.reshape(batch * seq, D_MODEL)
    xs = x_sample.transpose(1, 0, 2).reshape(nt * nb, D_MODEL)
    ms = nt * nb
    tm_p, tm_s = 512, 128
    cache_rope_t = cache_mla_rope.transpose(0, 1, 3, 2)

    def to_batch_major(a):
        return a.reshape(nt, nb, a.shape[-1]).transpose(1, 0, 2)

    lat_p, rope_p, lat_s, rope_s, conv_p, conv_s = [], [], [], [], [], []
    for layer in range(DEPTH):
        g_mix, b_mix = ln_mix_g[layer][None], ln_mix_b[layer][None]
        if layer < N_A_LAYERS:
            wa = _mla_weights(mla_w_in[layer], mla_g_q[layer], mla_g_kv[layer], mla_w_uq[layer], mla_w_uk[layer],
                              mla_w_uv[layer], mla_w_o[layer])
            ckv, kr, kcat, q = _mla_proj(xp, wa, tab_a_p, tm_p)
            ol = _mla_flash(q, kcat, batch, seq)
            xp = _mla_out(ol, wa, xp, g_mix, b_mix, tm_p)
            lat_p.append(ckv.reshape(batch, seq, KV_LORA))
            rope_p.append(kr[:, :ROPE_A].reshape(batch, seq, ROPE_A))

            ckv_s, kr_s, _, q_s = _mla_proj(xs, wa, tab_a_s, tm_s)
            q_b = q_s.reshape(nt, N_HEADS_A, nb, MLA_QK_W).transpose(2, 0, 1, 3).reshape(nb, nt * N_HEADS_A, MLA_QK_W)
            padrows = ((0, 0), (0, SUBLANES - nt), (0, 0))
            c_new = jnp.pad(to_batch_major(ckv_s), padrows)
            r_new = jnp.pad(to_batch_major(kr_s), padrows)
            o_b = _mla_sample_attn(page_table, q_b, c_new, r_new, cache_mla_latent, cache_rope_t, layer)
            ol_s = o_b.reshape(nb, nt, N_HEADS_A * KV_LORA).transpose(1, 0, 2).reshape(ms, N_HEADS_A * KV_LORA)
            xs = _mla_out(ol_s, wa, xs, g_mix, b_mix, tm_s)
            lat_s.append(to_batch_major(ckv_s))
            rope_s.append(to_batch_major(kr_s[:, :ROPE_A]))
        else:
            if layer == N_A_LAYERS:
                w_kv = dil_w_kv.astype(BF16)
                lens_p = tuple(min(w, seq) for w in WINDOWS)
                kv_res = _proj_rope(xp, w_kv, tab_b_p, QB_W, F32, 512, lane_blocks=True, windows=lens_p, seq=seq)
                kv_p = kv_res[0]
                new_win_k_prompt = jnp.concatenate(kv_res[1:1 + N_GROUPS], axis=-1).transpose(0, 3, 1, 2)
                new_win_v_prompt = jnp.concatenate(kv_res[1 + N_GROUPS:], axis=-1).transpose(0, 3, 1, 2)
                kv_s = _proj_rope(xs, w_kv, tab_b_s, QB_W, F32, tm_s)
                kv_s6 = kv_s.reshape(nt, nb, 2, N_GROUPS, HEADS_PER_GROUP, HEAD_DIM_B)
                lens_s = tuple(min(w, past_len) for w in WINDOWS)

                def new_rows_t(a):
                    a = a.transpose(1, 3, 4, 2, 0).reshape(nb, HEADS_PER_GROUP, HEAD_DIM_B, N_GROUPS * nt)
                    return jnp.pad(a, ((0, 0), (0, 0), (0, 0), (0, LANES - N_GROUPS * nt)))

                kn_t, vn_t = new_rows_t(kv_s6[:, :, 0]), new_rows_t(kv_s6[:, :, 1])
                ck_t = cache_win_k.transpose(0, 2, 3, 1)
                cv_t = cache_win_v.transpose(0, 2, 3, 1)
            bi = layer - N_A_LAYERS
            w_q = dil_w_q[bi].astype(BF16)
            w_o = dil_w_o[bi].astype(BF16)
            q_p = _proj_rope(xp, w_q, tab_b_p, QB_W, F32, 512, lane_blocks=True)
            outs, lses = zip(*[_dil_attn_group(q_p, kv_p, batch, seq, g) for g in range(N_GROUPS)])
            xp = _dil_merge(outs, lses, w_o, xp, g_mix, b_mix, 512)

            q_s = _proj_rope(xs, w_q, tab_b_s, QB_W, BF16, tm_s)
            q_g = q_s.reshape(nt, nb, N_GROUPS, HEADS_PER_GROUP, HEAD_DIM_B).transpose(1, 3, 2, 0, 4)
            q_g = jnp.pad(q_g, ((0, 0), (0, 0), (0, DS_ROWS // SUBLANES - N_GROUPS), (0, SUBLANES - nt), (0, 0)))
            q_g = q_g.reshape(nb, HEADS_PER_GROUP, DS_ROWS, HEAD_DIM_B)
            write_cache = layer == N_A_LAYERS
            res = _dil_sample(q_g, kn_t, vn_t, ck_t, cv_t, lens_s, nt, write_cache)
            if write_cache:
                new_win_k_sample = res[1].transpose(0, 3, 1, 2)
                new_win_v_sample = res[2].transpose(0, 3, 1, 2)
            y_s = res[0][:, :, :nt].transpose(2, 0, 1, 3).reshape(ms, GROUP_W)
            xs = _proj_ln(y_s, w_o, xs, g_mix, b_mix, tm_s)

        wf = dict(w_in=ffn_w_in[layer].astype(BF16), conv_w=ffn_conv_w[layer], conv_b=ffn_conv_b[layer][None],
                  w_out=ffn_w_out[layer].astype(BF16))
        g_ffn, b_ffn = ln_ffn_g[layer][None], ln_ffn_b[layer][None]
        xp, st_p = _ffn_prompt(xp, wf, g_ffn, b_ffn, seq, 512)
        xs, st_s = _ffn_sample(xs, state_conv[layer].transpose(1, 0, 2), wf, g_ffn, b_ffn, nb, nt)
        conv_p.append(st_p)
        conv_s.append(st_s.transpose(1, 0, 2))

    y_prompt = xp.reshape(batch, seq, D_MODEL)
    y_sample = xs.reshape(nt, nb, D_MODEL).transpose(1, 0, 2)
    return (y_prompt, y_sample, jnp.stack(lat_p, 0), jnp.stack(rope_p, 0), new_win_k_prompt, new_win_v_prompt,
            jnp.stack(conv_p, 0), jnp.stack(lat_s, 0), jnp.stack(rope_s, 0), new_win_k_sample, new_win_v_sample,
            jnp.stack(conv_s, 0))
```

```python
import functools

import numpy as np
import jax
import jax.numpy as jnp
from jax import lax
from jax.experimental import pallas as pl
from jax.experimental.pallas import tpu as pltpu

F32 = jnp.float32
BF16 = jnp.bfloat16

D_MODEL = 1024
DEPTH = 4
PAGE_SIZE = 128
N_A_LAYERS = DEPTH // 2
N_HEADS_A = 16
NOPE_A = 64
ROPE_A = 32
V_DIM_A = 64
Q_LORA = 256
KV_LORA = 128
MLA_SCALE = (NOPE_A + ROPE_A) ** -0.5
WINDOWS = (128, 512, 2048)
DILATIONS = (1, 4, 16)
N_GROUPS = 3
HEADS_PER_GROUP = 8
HEAD_DIM_B = 64
N_HEADS_B = N_GROUPS * HEADS_PER_GROUP
ROT_DIM_B = HEAD_DIM_B // 4
B_SCALE = HEAD_DIM_B ** -0.5
BLOCK_B = 128
ROPE_THETA = 500000.0
D_FF = 2816
CONV_W = 3
ALPHA = (2.0 * DEPTH) ** 0.25
LN_EPS = 1e-5
RMS_EPS = 1e-6
NEG_INF = -1e30
SQRT_HALF = 0.7071067811865476
LOG2_E = 1.4426950408889634
MLA_Q_SCALE = MLA_SCALE * LOG2_E

LANES = 128
SUBLANES = 8
VMEM_LIMIT_BYTES = 48 * 1024 * 1024

GROUP_W = HEADS_PER_GROUP * HEAD_DIM_B
QB_W = N_HEADS_B * HEAD_DIM_B
MLA_QK_W = 2 * LANES
FFN_CHUNK = 256


def _params(*sem):
    return pltpu.CompilerParams(dimension_semantics=sem, vmem_limit_bytes=VMEM_LIMIT_BYTES)


def _const_spec(shape):
    nd = len(shape)
    return pl.BlockSpec(shape, lambda *_: (0,) * nd)


def _dot(a, b):
    return jnp.dot(a, b, preferred_element_type=F32)


def _dot_nt(a, b):
    return lax.dot_general(a, b, (((1,), (1,)), ((), ())), preferred_element_type=F32)


def _layer_norm(z, g, b):
    mu = jnp.mean(z, axis=-1, keepdims=True)
    zc = z - mu
    var = jnp.mean(zc * zc, axis=-1, keepdims=True)
    return zc * lax.rsqrt(var + LN_EPS) * g + b


def _rms_norm(a, g):
    return a * lax.rsqrt(jnp.mean(a * a, axis=-1, keepdims=True) + RMS_EPS) * g


def _rope_lanes(v, cos, sin_a, sin_b, half):
    return (v * cos + pltpu.roll(v, LANES - half, 1) * sin_a + pltpu.roll(v, half, 1) * sin_b)


def _rope_tables(pos, rot_dim, period):
    half = rot_dim // 2
    inv = ROPE_THETA ** (-jnp.arange(half, dtype=F32) * 2.0 / rot_dim)
    ang = pos.astype(F32)[:, None] * inv[None, :]
    cos, sin = jnp.cos(ang), jnp.sin(ang)
    t = pos.shape[0]
    ones = jnp.ones((t, period - rot_dim), F32)
    zr = lambda n: jnp.zeros((t, n), F32)
    c = jnp.concatenate([cos, cos, ones], axis=1)
    sa = jnp.concatenate([-sin, zr(period - half)], axis=1)
    sb = jnp.concatenate([zr(half), sin, zr(period - rot_dim)], axis=1)
    rep = LANES // period
    return tuple(jnp.tile(a, (1, rep)) for a in (c, sa, sb))


def _mla_proj_kernel(x_ref, win_ref, gq_ref, gkv_ref, cos_ref, sa_ref, sb_ref, wn_ref, wr_ref, wuk_ref,
                     ckv_ref, kr_ref, kcat_ref, q_ref, *, tm):
    xb = x_ref[...].astype(BF16)
    a = _dot(xb, win_ref[...])
    cq = _rms_norm(a[:, :Q_LORA], gq_ref[...])
    ckv = _rms_norm(a[:, Q_LORA:Q_LORA + KV_LORA], gkv_ref[...])
    cos, sa, sb = cos_ref[...], sa_ref[...], sb_ref[...]
    kr = _rope_lanes(a[:, Q_LORA + KV_LORA:], cos, sa, sb, ROPE_A // 2)
    ckv_ref[...] = ckv
    kr_ref[...] = kr
    kcat_ref[:, :LANES] = ckv.astype(BF16)
    kcat_ref[:, LANES:] = kr.astype(BF16)
    cqb = cq.astype(BF16)
    qn = _dot(cqb, wn_ref[...])
    qr = _dot(cqb, wr_ref[...])
    for p in range(N_HEADS_A // 2):
        ql2 = _dot(qn[:, p * LANES:(p + 1) * LANES].astype(BF16), wuk_ref[p])
        for hh in range(2):
            h = 2 * p + hh
            ql = (ql2[:, hh * LANES:(hh + 1) * LANES] * MLA_Q_SCALE).astype(BF16)
            rr = _rope_lanes(qr[:, h * LANES:(h + 1) * LANES], cos, sa, sb, ROPE_A // 2)
            rr = (rr * MLA_Q_SCALE).astype(BF16)
            for j in range(tm // LANES):
                q_ref[j, h, :, :LANES] = ql[j * LANES:(j + 1) * LANES]
                q_ref[j, h, :, LANES:] = rr[j * LANES:(j + 1) * LANES]


def _mla_proj(x, w, tables, tm):
    m = x.shape[0]
    cos, sa, sb = tables
    nper = cos.shape[0] // tm
    tab_spec = pl.BlockSpec((tm, LANES), lambda i: (i % nper, 0))
    row = lambda n: pl.BlockSpec((tm, n), lambda i: (i, 0))
    return pl.pallas_call(
        functools.partial(_mla_proj_kernel, tm=tm),
        grid=(m // tm,),
        in_specs=[row(D_MODEL), _const_spec(w["w_in"].shape), _const_spec((1, Q_LORA)), _const_spec((1, KV_LORA)),
                  tab_spec, tab_spec, tab_spec, _const_spec(w["w_nope"].shape), _const_spec(w["w_rope"].shape),
                  _const_spec(w["w_uk"].shape)],
        out_specs=[row(KV_LORA), row(LANES), row(MLA_QK_W),
                   pl.BlockSpec((tm // LANES, N_HEADS_A, LANES, MLA_QK_W), lambda i: (i, 0, 0, 0))],
        out_shape=[jax.ShapeDtypeStruct((m, KV_LORA), F32), jax.ShapeDtypeStruct((m, LANES), F32),
                   jax.ShapeDtypeStruct((m, MLA_QK_W), BF16),
                   jax.ShapeDtypeStruct((m // LANES, N_HEADS_A, LANES, MLA_QK_W), BF16)],
        compiler_params=_params("parallel"),
        name="mla_proj",
    )(x, w["w_in"], w["g_q"], w["g_kv"], cos, sa, sb, w["w_nope"], w["w_rope"], w["w_uk"])


MLA_TQ = 256
MLA_TK = 512
MLA_HEADS_PER_DOT = 4
MLA_SUB = MLA_TQ // LANES


def _mla_flash_kernel(qi_ref, kj_ref, q_ref, k_ref, o_ref, m_ref, acc_ref):
    t = pl.program_id(1)
    qi = qi_ref[t]
    kj = kj_ref[t]
    last = (qi * MLA_TQ + MLA_TQ - 1) // MLA_TK

    @pl.when(kj == 0)
    def _():
        m_ref[...] = jnp.full(m_ref.shape, NEG_INF, F32)
        acc_ref[...] = jnp.zeros(acc_ref.shape, F32)

    k = k_ref[...]
    v_ones = jnp.concatenate([k[:, :KV_LORA], jnp.ones((MLA_TK, LANES), BF16)], axis=1)

    nrow = MLA_HEADS_PER_DOT * LANES
    groups = N_HEADS_A // MLA_HEADS_PER_DOT

    def step(masked):
        for sb in range(MLA_SUB):
            if masked:
                q_pos = qi * MLA_TQ + sb * LANES + (lax.broadcasted_iota(jnp.int32, (nrow, MLA_TK), 0) & (LANES - 1))
                k_pos = kj * MLA_TK + lax.broadcasted_iota(jnp.int32, (nrow, MLA_TK), 1)
                visible = k_pos <= q_pos
            for hg in range(groups):
                rows = pl.ds((sb * groups + hg) * nrow, nrow)
                q = q_ref[sb, hg * MLA_HEADS_PER_DOT:(hg + 1) * MLA_HEADS_PER_DOT].reshape(nrow, MLA_QK_W)
                s = _dot_nt(q, k)
                if masked:
                    s = jnp.where(visible, s, NEG_INF)
                m_prev = m_ref[rows, :]
                m_new = jnp.maximum(m_prev, jnp.max(s, axis=-1, keepdims=True))
                alpha = jnp.exp2(m_prev - m_new)
                p = jnp.exp2(s - jnp.concatenate([m_new] * (MLA_TK // LANES), axis=1))
                pv = _dot(p.astype(BF16), v_ones)
                acc_ref[rows, :] = jnp.concatenate([alpha, alpha], axis=1) * acc_ref[rows, :] + pv
                m_ref[rows, :] = m_new

    @pl.when(kj != last)
    def _():
        step(False)

    @pl.when(kj == last)
    def _():
        step(True)
        for sb in range(MLA_SUB):
            for h in range(N_HEADS_A):
                a = acc_ref[pl.ds((sb * N_HEADS_A + h) * LANES, LANES), :]
                o_ref[sb * LANES:(sb + 1) * LANES, h * KV_LORA:(h + 1) * KV_LORA] = (
                    a[:, :KV_LORA] / a[:, KV_LORA:]).astype(BF16)


def _mla_flash(q, kcat, batch, seq):
    nq = seq // MLA_TQ
    nk = seq // MLA_TK
    pairs = [(i, j) for i in range(nq) for j in range((i * MLA_TQ + MLA_TQ - 1) // MLA_TK + 1)]
    qi = jnp.asarray(np.array([p[0] for p in pairs], np.int32))
    kj = jnp.asarray(np.array([p[1] for p in pairs], np.int32))
    rows = N_HEADS_A * MLA_TQ
    return pl.pallas_call(
        _mla_flash_kernel,
        grid_spec=pltpu.PrefetchScalarGridSpec(
            num_scalar_prefetch=2,
            grid=(batch, len(pairs)),
            in_specs=[pl.BlockSpec((MLA_SUB, N_HEADS_A, LANES, MLA_QK_W), lambda b, t, qi, kj: (b * nq + qi[t], 0, 0, 0)),
                      pl.BlockSpec((MLA_TK, MLA_QK_W), lambda b, t, qi, kj: (b * nk + kj[t], 0))],
            out_specs=pl.BlockSpec((MLA_TQ, N_HEADS_A * KV_LORA), lambda b, t, qi, kj: (b * nq + qi[t], 0)),
            scratch_shapes=[pltpu.VMEM((rows, LANES), F32), pltpu.VMEM((rows, KV_LORA + LANES), F32)],
        ),
        out_shape=jax.ShapeDtypeStruct((batch * seq, N_HEADS_A * KV_LORA), BF16),
        compiler_params=_params("parallel", "arbitrary"),
        name="mla_flash",
    )(qi, kj, q, kcat)


MLA_SAMPLE_SEQS = 2


def _mla_sample_kernel(pt_ref, q_ref, cn_ref, rn_ref, lat_hbm, rope_hbm, o_ref, lat_buf, rope_buf, sem,
                       *, layer, n_pages, n_new):
    step = pl.program_id(0)
    n_steps = pl.num_programs(0)
    slot = step % 2
    seqs = range(MLA_SAMPLE_SEQS)

    def lat_copy(st, sl, u, j):
        page = pt_ref[st * MLA_SAMPLE_SEQS + u, j]
        return pltpu.make_async_copy(lat_hbm.at[layer, page], lat_buf.at[sl, u, j], sem.at[sl, 0])

    def rope_copy(st, sl, u, j):
        page = pt_ref[st * MLA_SAMPLE_SEQS + u, j]
        return pltpu.make_async_copy(rope_hbm.at[layer, page],
                                     rope_buf.at[sl, u, :, pl.ds(j * PAGE_SIZE, PAGE_SIZE)], sem.at[sl, 1])

    def start_all(st, sl):
        for u in seqs:
            for j in range(n_pages):
                lat_copy(st, sl, u, j).start()
                rope_copy(st, sl, u, j).start()

    @pl.when(step == 0)
    def _():
        start_all(0, 0)

    @pl.when(step + 1 < n_steps)
    def _():
        start_all(step + 1, 1 - slot)

    for u in seqs:
        for j in range(n_pages):
            lat_copy(step, slot, u, j).wait()
            rope_copy(step, slot, u, j).wait()

    n_keys = n_pages * PAGE_SIZE
    lats, cns, scores = [], [], []
    for u in seqs:
        lat = lat_buf[slot, u].reshape(n_keys, KV_LORA).astype(BF16)
        rope_t = rope_buf[slot, u].astype(BF16)
        q = q_ref[u]
        q_lat = q[:, :KV_LORA]
        cn = cn_ref[u].astype(BF16)
        rn = rn_ref[u].astype(BF16)
        s = _dot_nt(q_lat, lat) + _dot(q[:, KV_LORA:KV_LORA + ROPE_A], rope_t)
        s_new = _dot_nt(q_lat, cn) + _dot_nt(q[:, KV_LORA:], rn)
        lats.append(lat)
        cns.append(cn)
        scores.append((s, s_new))
    probs = []
    for s, s_new in scores:
        t_q = lax.broadcasted_iota(jnp.int32, s_new.shape, 0) // N_HEADS_A
        j_k = lax.broadcasted_iota(jnp.int32, s_new.shape, 1)
        s_new = jnp.where(j_k <= t_q, s_new, NEG_INF)
        m = jnp.maximum(jnp.max(s, axis=-1, keepdims=True), jnp.max(s_new, axis=-1, keepdims=True))
        e = jnp.exp2(s - m)
        e_new = jnp.exp2(s_new - m)
        den = jnp.sum(e, axis=-1, keepdims=True) + jnp.sum(e_new, axis=-1, keepdims=True)
        probs.append(((e / den).astype(BF16), (e_new / den).astype(BF16)))
    for u in seqs:
        o_ref[u] = (_dot(probs[u][0], lats[u]) + _dot(probs[u][1], cns[u])).astype(BF16)


def _mla_sample_attn(page_table, q, c_new, r_new, cache_lat, cache_rope_t, layer):
    nb, n_pages = page_table.shape
    rows = q.shape[1]
    n_new = rows // N_HEADS_A
    ns = MLA_SAMPLE_SEQS
    assert nb % ns == 0
    return pl.pallas_call(
        functools.partial(_mla_sample_kernel, layer=layer, n_pages=n_pages, n_new=n_new),
        grid_spec=pltpu.PrefetchScalarGridSpec(
            num_scalar_prefetch=1,
            grid=(nb // ns,),
            in_specs=[pl.BlockSpec((ns, rows, MLA_QK_W), lambda b, pt: (b, 0, 0)),
                      pl.BlockSpec((ns, SUBLANES, KV_LORA), lambda b, pt: (b, 0, 0)),
                      pl.BlockSpec((ns, SUBLANES, LANES), lambda b, pt: (b, 0, 0)),
                      pl.BlockSpec(memory_space=pl.ANY),
                      pl.BlockSpec(memory_space=pl.ANY)],
            out_specs=pl.BlockSpec((ns, rows, KV_LORA), lambda b, pt: (b, 0, 0)),
            scratch_shapes=[pltpu.VMEM((2, ns, n_pages, PAGE_SIZE, KV_LORA), F32),
                            pltpu.VMEM((2, ns, ROPE_A, n_pages * PAGE_SIZE), F32),
                            pltpu.SemaphoreType.DMA((2, 2))],
        ),
        out_shape=jax.ShapeDtypeStruct((nb, rows, KV_LORA), BF16),
        compiler_params=_params("arbitrary"),
        name="mla_sample_attn",
    )(page_table, q, c_new, r_new, cache_lat, cache_rope_t)


def _mla_out_kernel(ol_ref, wuv_ref, wo_ref, x_ref, g_ref, b_ref, y_ref):
    parts = [_dot(ol_ref[:, p * 2 * KV_LORA:(p + 1) * 2 * KV_LORA], wuv_ref[p]) for p in range(N_HEADS_A // 2)]
    o = jnp.concatenate(parts, axis=1).astype(BF16)
    mix = _dot(o, wo_ref[...])
    y_ref[...] = _layer_norm(ALPHA * x_ref[...] + mix, g_ref[...], b_ref[...])


def _mla_out(ol, w, x, g, b, tm):
    m = x.shape[0]
    row = lambda n: pl.BlockSpec((tm, n), lambda i: (i, 0))
    return pl.pallas_call(
        _mla_out_kernel,
        grid=(m // tm,),
        in_specs=[row(N_HEADS_A * KV_LORA), _const_spec(w["w_uv"].shape), _const_spec(w["w_o"].shape),
                  row(D_MODEL), _const_spec((1, D_MODEL)), _const_spec((1, D_MODEL))],
        out_specs=row(D_MODEL),
        out_shape=jax.ShapeDtypeStruct((m, D_MODEL), F32),
        compiler_params=_params("parallel"),
        name="mla_out",
    )(ol, w["w_uv"], w["w_o"], x, g, b)


def _proj_ln_kernel(a_ref, wo_ref, x_ref, g_ref, b_ref, y_ref):
    mix = _dot(a_ref[...].astype(BF16), wo_ref[...])
    y_ref[...] = _layer_norm(ALPHA * x_ref[...] + mix, g_ref[...], b_ref[...])


def _proj_ln(a, wo, x, g, b, tm):
    m = x.shape[0]
    row = lambda n: pl.BlockSpec((tm, n), lambda i: (i, 0))
    return pl.pallas_call(
        _proj_ln_kernel,
        grid=(m // tm,),
        in_specs=[row(a.shape[1]), _const_spec(wo.shape), row(D_MODEL), _const_spec((1, D_MODEL)),
                  _const_spec((1, D_MODEL))],
        out_specs=row(D_MODEL),
        out_shape=jax.ShapeDtypeStruct((m, D_MODEL), F32),
        compiler_params=_params("parallel"),
        name="proj_ln",
    )(a, wo, x, g, b)


def _dil_merge_kernel(o0, o1, o2, l0, l1, l2, wo_ref, x_ref, g_ref, b_ref, y_ref):
    wide = lambda ref: jnp.concatenate([ref[j] for j in range(ref.shape[0])], axis=1)
    la, lb, lc = wide(l0), wide(l1), wide(l2)
    m = jnp.maximum(jnp.maximum(la, lb), lc)
    ea, eb, ec = jnp.exp(la - m), jnp.exp(lb - m), jnp.exp(lc - m)
    den = ea + eb + ec
    y = (ea / den) * wide(o0) + (eb / den) * wide(o1) + (ec / den) * wide(o2)
    mix = _dot(y.astype(BF16), wo_ref[...])
    y_ref[...] = _layer_norm(ALPHA * x_ref[...] + mix, g_ref[...], b_ref[...])


def _dil_merge(outs, lses, wo, x, g, b, tm):
    m = x.shape[0]
    row = lambda n: pl.BlockSpec((tm, n), lambda i: (i, 0))
    return pl.pallas_call(
        _dil_merge_kernel,
        grid=(m // tm,),
        in_specs=[pl.BlockSpec((outs[0].shape[0], tm, LANES), lambda i: (0, i, 0))] * 6
        + [_const_spec(wo.shape), row(D_MODEL), _const_spec((1, D_MODEL)), _const_spec((1, D_MODEL))],
        out_specs=row(D_MODEL),
        out_shape=jax.ShapeDtypeStruct((m, D_MODEL), F32),
        compiler_params=_params("parallel"),
        name="dil_merge",
    )(*outs, *lses, wo, x, g, b)


def _gelu_gate(acc, val):
    return 0.5 * acc * (1.0 + lax.erf(acc * SQRT_HALF)) * val


def _ffn_prompt_kernel(x_ref, xh_ref, win_ref, cw_ref, cb_ref, wout_ref, g_ref, b_ref, y_ref, st_ref,
                       *, tm, tiles_per_seq):
    i = pl.program_id(0)
    x = x_ref[...]
    xb = x.astype(BF16)
    xhb = xh_ref[...].astype(BF16)
    keep = ((i % tiles_per_seq) != 0).astype(F32)
    acc = jnp.zeros((tm, D_MODEL), F32)
    n_chunks = D_FF // FFN_CHUNK

    def up_proj(c):
        lo = c * FFN_CHUNK
        wg = win_ref[:, lo:lo + FFN_CHUNK]
        return (_dot(xb, wg), _dot(xb, win_ref[:, D_FF + lo:D_FF + lo + FFN_CHUNK]), _dot(xhb, wg))

    nxt = up_proj(0)
    for c in range(n_chunks):
        lo = c * FFN_CHUNK
        gate, val, gh = nxt
        if c + 1 < n_chunks:
            nxt = up_proj(c + 1)
        ext = jnp.concatenate([gh * keep, gate], axis=0)
        g1 = ext[SUBLANES - 1:SUBLANES - 1 + tm]
        g2 = ext[SUBLANES - 2:SUBLANES - 2 + tm]
        a = cb_ref[:, lo:lo + FFN_CHUNK] + cw_ref[0:1, lo:lo + FFN_CHUNK] * g2
        a = a + cw_ref[1:2, lo:lo + FFN_CHUNK] * g1
        a = a + cw_ref[2:3, lo:lo + FFN_CHUNK] * gate
        h = _gelu_gate(a, val).astype(BF16)
        acc = acc + _dot(h, wout_ref[lo:lo + FFN_CHUNK, :])
        st_ref[0, :, lo:lo + FFN_CHUNK] = gate[tm - SUBLANES:]
    y_ref[...] = _layer_norm(ALPHA * x + acc, g_ref[...], b_ref[...])


def _ffn_prompt(x, w, g, b, seq, tm):
    m = x.shape[0]
    tiles_per_seq = seq // tm
    hb = tm // SUBLANES
    row = lambda n: pl.BlockSpec((tm, n), lambda i: (i, 0))
    y, st = pl.pallas_call(
        functools.partial(_ffn_prompt_kernel, tm=tm, tiles_per_seq=tiles_per_seq),
        grid=(m // tm,),
        in_specs=[row(D_MODEL),
                  pl.BlockSpec((SUBLANES, D_MODEL), lambda i: (jnp.maximum(i * hb - 1, 0), 0)),
                  _const_spec(w["w_in"].shape), _const_spec((CONV_W, D_FF)), _const_spec((1, D_FF)),
                  _const_spec(w["w_out"].shape), _const_spec((1, D_MODEL)), _const_spec((1, D_MODEL))],
        out_specs=[row(D_MODEL), pl.BlockSpec((1, SUBLANES, D_FF), lambda i: (i // tiles_per_seq, 0, 0))],
        out_shape=[jax.ShapeDtypeStruct((m, D_MODEL), F32),
                   jax.ShapeDtypeStruct((m // seq, SUBLANES, D_FF), F32)],
        compiler_params=_params("arbitrary"),
        name="ffn_prompt",
    )(x, x, w["w_in"], w["conv_w"], w["conv_b"], w["w_out"], g, b)
    return y, st[:, SUBLANES - (CONV_W - 1):]


def _ffn_sample_kernel(x_ref, s_ref, win_ref, cw_ref, cb_ref, wout_ref, g_ref, b_ref, y_ref, st_ref, *, nb, nt):
    x = x_ref[...]
    xb = x.astype(BF16)
    acc = jnp.zeros((nt * nb, D_MODEL), F32)
    for c in range(D_FF // FFN_CHUNK):
        lo = c * FFN_CHUNK
        gate = _dot(xb, win_ref[:, lo:lo + FFN_CHUNK])
        val = _dot(xb, win_ref[:, D_FF + lo:D_FF + lo + FFN_CHUNK])
        s0 = s_ref[0, :, lo:lo + FFN_CHUNK]
        s1 = s_ref[1, :, lo:lo + FFN_CHUNK]
        g1 = jnp.concatenate([s1, gate[:(nt - 1) * nb]], axis=0)
        g2 = jnp.concatenate([s0, s1, gate[:(nt - 2) * nb]], axis=0)
        a = cb_ref[:, lo:lo + FFN_CHUNK] + cw_ref[0:1, lo:lo + FFN_CHUNK] * g2
        a = a + cw_ref[1:2, lo:lo + FFN_CHUNK] * g1
        a = a + cw_ref[2:3, lo:lo + FFN_CHUNK] * gate
        h = _gelu_gate(a, val).astype(BF16)
        acc = acc + _dot(h, wout_ref[lo:lo + FFN_CHUNK, :])
        st_ref[0, :, lo:lo + FFN_CHUNK] = gate[(nt - 2) * nb:(nt - 1) * nb]
        st_ref[1, :, lo:lo + FFN_CHUNK] = gate[(nt - 1) * nb:]
    y_ref[...] = _layer_norm(ALPHA * x + acc, g_ref[...], b_ref[...])


def _ffn_sample(x, state, w, g, b, nb, nt):
    m = x.shape[0]
    return pl.pallas_call(
        functools.partial(_ffn_sample_kernel, nb=nb, nt=nt),
        out_shape=[jax.ShapeDtypeStruct((m, D_MODEL), F32), jax.ShapeDtypeStruct((CONV_W - 1, nb, D_FF), F32)],
        compiler_params=pltpu.CompilerParams(vmem_limit_bytes=VMEM_LIMIT_BYTES),
        name="ffn_sample",
    )(x, state, w["w_in"], w["conv_w"], w["conv_b"], w["w_out"], g, b)


PROJ_CHUNK = 512


def _proj_rope_kernel(x_ref, w_ref, cos_ref, sa_ref, sb_ref, o_ref, *win_refs, n, rope_cols, lane_blocks, tm,
                      windows, tiles_per_seq):
    xb = x_ref[...].astype(BF16)
    cos, sa, sb = cos_ref[...], sa_ref[...], sb_ref[...]
    is_last_tile = (pl.program_id(0) % tiles_per_seq) == tiles_per_seq - 1 if windows else None
    for c in range(n // PROJ_CHUNK):
        lo = c * PROJ_CHUNK
        y = _dot(xb, w_ref[:, lo:lo + PROJ_CHUNK])
        for k in range(PROJ_CHUNK // LANES):
            slab = lo // LANES + k
            yk = y[:, k * LANES:(k + 1) * LANES]
            if lo < rope_cols:
                yk = _rope_lanes(yk, cos, sa, sb, ROT_DIM_B // 2)
            if lane_blocks:
                o_ref[slab] = yk.astype(o_ref.dtype)
            else:
                o_ref[:, lo + k * LANES:lo + (k + 1) * LANES] = yk.astype(o_ref.dtype)
            if windows:
                kv_sel, rest = divmod(slab, N_GROUPS * GROUP_SLABS)
                g, j = divmod(rest, GROUP_SLABS)
                dst = win_refs[kv_sel * N_GROUPS + g]
                width = min(windows[g], tm)

                def emit(dst=dst, j=j, width=width, yk=yk):
                    t = jnp.transpose(yk[tm - width:])
                    dst[0, 2 * j:2 * j + 2] = t.reshape(2, HEAD_DIM_B, width)

                if windows[g] >= tiles_per_seq * tm:
                    emit()
                else:
                    pl.when(is_last_tile)(emit)


def _proj_rope(x, w, tables, rope_cols, out_dtype, tm, lane_blocks=False, windows=None, seq=None):
    m = x.shape[0]
    n = w.shape[1]
    cos, sa, sb = tables
    nper = cos.shape[0] // tm
    tab_spec = pl.BlockSpec((tm, LANES), lambda i: (i % nper, 0))
    if lane_blocks:
        out_specs = [pl.BlockSpec((n // LANES, tm, LANES), lambda i: (0, i, 0))]
        out_shape = [jax.ShapeDtypeStruct((n // LANES, m, LANES), out_dtype)]
    else:
        out_specs = [pl.BlockSpec((tm, n), lambda i: (i, 0))]
        out_shape = [jax.ShapeDtypeStruct((m, n), out_dtype)]
    tiles_per_seq = None
    if windows:
        tiles_per_seq = seq // tm
        assert all(wl == seq or wl <= tm for wl in windows) and n == 2 * QB_W
        for _ in range(2):
            for wl in windows:
                if wl == seq:
                    spec = pl.BlockSpec((1, HEADS_PER_GROUP, HEAD_DIM_B, tm),
                                        lambda i: (i // tiles_per_seq, 0, 0, i % tiles_per_seq))
                else:
                    spec = pl.BlockSpec((1, HEADS_PER_GROUP, HEAD_DIM_B, wl), lambda i: (i // tiles_per_seq, 0, 0, 0))
                out_specs.append(spec)
                out_shape.append(jax.ShapeDtypeStruct((m // seq, HEADS_PER_GROUP, HEAD_DIM_B, wl), F32))
    res = pl.pallas_call(
        functools.partial(_proj_rope_kernel, n=n, rope_cols=rope_cols, lane_blocks=lane_blocks, tm=tm,
                          windows=windows, tiles_per_seq=tiles_per_seq),
        grid=(m // tm,),
        in_specs=[pl.BlockSpec((tm, D_MODEL), lambda i: (i, 0)), _const_spec(w.shape), tab_spec, tab_spec, tab_spec],
        out_specs=out_specs,
        out_shape=out_shape,
        compiler_params=_params("arbitrary" if windows else "parallel"),
        name="proj_rope",
    )(x, w, cos, sa, sb)
    return res if windows else res[0]


GROUP_SLABS = GROUP_W // LANES


def _dil_attn_kernel(q_ref, k_ref, v_ref, o_ref, l_ref, *, d, span):
    seq = q_ref.shape[2]
    nblk = seq // (BLOCK_B * d)
    has_prev = nblk > 1
    nk = 2 * BLOCK_B if has_prev else BLOCK_B
    a = lax.broadcasted_iota(jnp.int32, (BLOCK_B, nk), 0)
    c = lax.broadcasted_iota(jnp.int32, (BLOCK_B, nk), 1)
    diff = a + (nk - BLOCK_B) - c
    in_band = (diff >= 0) & (diff <= span)

    r_low_count = min(d, SUBLANES)

    def phase_rows(ref, n, r):
        r_hi, r_lo = r
        start = pl.multiple_of(n * (BLOCK_B * d) + r_hi * SUBLANES, SUBLANES) + r_lo
        return jnp.concatenate([ref[j, 0, pl.ds(start, BLOCK_B, stride=d), :] for j in range(GROUP_SLABS)], axis=1)

    def body(it, carry, r_lo):
        r = (it // nblk, r_lo)
        n = it % nblk
        q = phase_rows(q_ref, n, r).astype(BF16)
        if has_prev:
            n_prev = jnp.maximum(n - 1, 0)
            k = jnp.concatenate([phase_rows(k_ref, n_prev, r), phase_rows(k_ref, n, r)], axis=0).astype(BF16)
            v = jnp.concatenate([phase_rows(v_ref, n_prev, r), phase_rows(v_ref, n, r)], axis=0).astype(BF16)
            mask = in_band & ((n - 1) * BLOCK_B + c >= 0)
        else:
            k = phase_rows(k_ref, n, r).astype(BF16)
            v = phase_rows(v_ref, n, r).astype(BF16)
            mask = in_band
        head = lambda a, h: a[:, h * HEAD_DIM_B:(h + 1) * HEAD_DIM_B]
        scores = [_dot_nt(head(q, h), head(k, h)) for h in range(HEADS_PER_GROUP)]
        probs, lses = [], []
        for s in scores:
            s = jnp.where(mask, s * B_SCALE, NEG_INF)
            m = jnp.max(s, axis=-1, keepdims=True)
            e = jnp.exp(s - m)
            den = jnp.sum(e, axis=-1, keepdims=True)
            probs.append((e / den).astype(BF16))
            lses.append(jnp.broadcast_to(m + jnp.log(den), (BLOCK_B, HEAD_DIM_B)))
        outs = [_dot(probs[h], head(v, h)) for h in range(HEADS_PER_GROUP)]
        start = pl.multiple_of(n * (BLOCK_B * d) + r[0] * SUBLANES, SUBLANES) + r_lo
        rows = pl.ds(start, BLOCK_B, stride=d)
        for j in range(GROUP_SLABS):
            o_ref[j, 0, rows, :] = jnp.concatenate(outs[2 * j:2 * j + 2], axis=1)
            l_ref[j, 0, rows, :] = jnp.concatenate(lses[2 * j:2 * j + 2], axis=1)
        return carry

    for r_lo in range(r_low_count):
        trips = (d // r_low_count) * nblk
        lax.fori_loop(0, trips, functools.partial(body, r_lo=r_lo), 0, unroll=2 if trips > 2 else 1)


def _dil_attn_group(q, kv, batch, seq, g):
    d = DILATIONS[g]
    span = WINDOWS[g] // d
    nq = QB_W // GROUP_W
    blk = (GROUP_SLABS, 1, seq, LANES)
    col = lambda j: pl.BlockSpec(blk, lambda b: (j, b, 0, 0))
    shape = jax.ShapeDtypeStruct((GROUP_SLABS, batch, seq, LANES), F32)
    q4 = q.reshape(q.shape[0], batch, seq, LANES)
    kv4 = kv.reshape(kv.shape[0], batch, seq, LANES)
    o, l = pl.pallas_call(
        functools.partial(_dil_attn_kernel, d=d, span=span),
        grid=(batch,),
        in_specs=[col(g), col(g), col(nq + g)],
        out_specs=[col(0), col(0)],
        out_shape=[shape, shape],
        compiler_params=_params("parallel"),
        name=f"dil_attn_g{g}",
    )(q4, kv4, kv4)
    return o.reshape(GROUP_SLABS, batch * seq, LANES), l.reshape(GROUP_SLABS, batch * seq, LANES)


DS_HEADS = 4
DS_ROWS = 4 * SUBLANES
DS_NEW = 2 * SUBLANES


def _dil_sample_bias(lens, n_new):
    bias = np.full((DS_ROWS, sum(lens)), NEG_INF, np.float32)
    bias_new = np.full((DS_ROWS, DS_NEW), NEG_INF, np.float32)
    off = 0
    for g, ln in enumerate(lens):
        d = DILATIONS[g]
        span = WINDOWS[g] // d
        for t in range(n_new):
            rel = ln + t - np.arange(ln)
            ok = (rel % d == 0) & (rel // d <= span)
            bias[g * SUBLANES + t, off:off + ln] = np.where(ok, 0.0, NEG_INF)
            for t2 in range(t + 1):
                if (t - t2) % d == 0 and (t - t2) // d <= span:
                    bias_new[g * SUBLANES + t, n_new * g + t2] = 0.0
        off += ln
    return bias, bias_new


def _shift_append(src, new_rows, dst_ref, hh, lens, n_new):
    hd = src.shape[0]
    pad = jnp.concatenate([new_rows, jnp.zeros((new_rows.shape[0], LANES - hd), F32)], axis=1)
    pad = jnp.concatenate([pad, jnp.zeros((LANES - new_rows.shape[0], LANES), F32)], axis=0)
    new_t = jnp.transpose(pad)[:hd]
    nblk = src.shape[1] // LANES
    ends = {}
    off = 0
    for g, ln in enumerate(lens):
        off += ln
        ends[off // LANES - 1] = g
    lane = lax.broadcasted_iota(jnp.int32, (src.shape[0], LANES), 1)
    rolled = [pltpu.roll(src[:, cb * LANES:(cb + 1) * LANES], LANES - n_new, 1) for cb in range(nblk)]
    for cb in range(nblk):
        if cb in ends:
            tail = pltpu.roll(new_t, LANES - n_new - n_new * ends[cb], 1)
        else:
            tail = rolled[cb + 1]
        dst_ref[0, hh, :, cb * LANES:(cb + 1) * LANES] = jnp.where(lane < LANES - n_new, rolled[cb], tail)


def _dil_sample_kernel(q_ref, kn_ref, vn_ref, bias_ref, bn_ref, k_ref, v_ref, *out_refs, lens, n_new, write_cache):
    y_ref = out_refs[0]
    bias = bias_ref[...]
    bias_new = bn_ref[...]
    heads = range(DS_HEADS)
    scores = [(_dot(q_ref[0, hh], k_ref[0, hh].astype(BF16)), _dot_nt(q_ref[0, hh], kn_ref[0, hh].astype(BF16)))
              for hh in heads]
    probs, lses = [], []
    for s, s_new in scores:
        s = s * B_SCALE + bias
        s_new = s_new * B_SCALE + bias_new
        m = jnp.maximum(jnp.max(s, axis=-1, keepdims=True), jnp.max(s_new, axis=-1, keepdims=True))
        e = jnp.exp(s - m)
        e_new = jnp.exp(s_new - m)
        den = jnp.sum(e, axis=-1, keepdims=True) + jnp.sum(e_new, axis=-1, keepdims=True)
        probs.append(((e / den).astype(BF16), (e_new / den).astype(BF16)))
        lses.append(m + jnp.log(den))
    outs = [_dot_nt(probs[hh][0], v_ref[0, hh].astype(BF16)) + _dot(probs[hh][1], vn_ref[0, hh].astype(BF16))
            for hh in heads]
    grp = lambda a, g: a[g * SUBLANES:(g + 1) * SUBLANES]
    for hh in heads:
        lse, o = lses[hh], outs[hh]
        mm = jnp.maximum(jnp.maximum(grp(lse, 0), grp(lse, 1)), grp(lse, 2))
        ws = [jnp.exp(grp(lse, g) - mm) for g in range(N_GROUPS)]
        wsum = ws[0] + ws[1] + ws[2]
        y_ref[0, hh] = sum((ws[g] / wsum) * grp(o, g) for g in range(N_GROUPS))
    if write_cache:
        for hh in heads:
            _shift_append(k_ref[0, hh], kn_ref[0, hh], out_refs[1], hh, lens, n_new)
            _shift_append(v_ref[0, hh], vn_ref[0, hh], out_refs[2], hh, lens, n_new)


def _dil_sample(q, kn_t, vn_t, ck_t, cv_t, lens, n_new, write_cache):
    nb, hp, hd, rows = ck_t.shape
    assert all(ln % LANES == 0 for ln in lens) and sum(lens) == rows and N_GROUPS * n_new <= DS_NEW
    bias, bias_new = _dil_sample_bias(lens, n_new)
    cache_spec = pl.BlockSpec((1, DS_HEADS, hd, rows), lambda b, j: (b, j, 0, 0))
    new_spec = pl.BlockSpec((1, DS_HEADS, DS_NEW, hd), lambda b, j: (b, j, 0, 0))
    q_spec = pl.BlockSpec((1, DS_HEADS, DS_ROWS, hd), lambda b, j: (b, j, 0, 0))
    y_spec = pl.BlockSpec((1, DS_HEADS, SUBLANES, hd), lambda b, j: (b, j, 0, 0))
    out_specs = [y_spec]
    out_shape = [jax.ShapeDtypeStruct((nb, hp, SUBLANES, hd), F32)]
    if write_cache:
        out_specs += [cache_spec, cache_spec]
        out_shape += [jax.ShapeDtypeStruct(ck_t.shape, F32)] * 2
    return pl.pallas_call(
        functools.partial(_dil_sample_kernel, lens=lens, n_new=n_new, write_cache=write_cache),
        grid=(nb, hp // DS_HEADS),
        in_specs=[q_spec, new_spec, new_spec, _const_spec(bias.shape), _const_spec(bias_new.shape),
                  cache_spec, cache_spec],
        out_specs=out_specs,
        out_shape=out_shape,
        compiler_params=_params("parallel", "parallel"),
        name="dil_sample_update" if write_cache else "dil_sample",
    )(q, kn_t, vn_t, jnp.asarray(bias), jnp.asarray(bias_new), ck_t, cv_t)


def _mla_weights(w_in, g_q, g_kv, w_uq, w_uk, w_uv, w_o):
    w_in_p = jnp.pad(w_in, ((0, 0), (0, 4 * LANES - w_in.shape[1]))).astype(BF16)
    uq = w_uq.reshape(Q_LORA, N_HEADS_A, NOPE_A + ROPE_A)
    w_nope = uq[:, :, :NOPE_A].reshape(Q_LORA, N_HEADS_A * NOPE_A).astype(BF16)
    w_rope = jnp.pad(uq[:, :, NOPE_A:], ((0, 0), (0, 0), (0, LANES - ROPE_A)))
    w_rope = w_rope.reshape(Q_LORA, N_HEADS_A * LANES).astype(BF16)
    ukt = w_uk.transpose(1, 2, 0).reshape(N_HEADS_A // 2, 2, NOPE_A, KV_LORA)
    z = jnp.zeros_like(ukt[:, 0])
    uk_bd = jnp.concatenate([jnp.concatenate([ukt[:, 0], z], axis=2),
                             jnp.concatenate([z, ukt[:, 1]], axis=2)], axis=1).astype(BF16)
    uvt = w_uv.transpose(1, 0, 2).reshape(N_HEADS_A // 2, 2, KV_LORA, V_DIM_A)
    z = jnp.zeros_like(uvt[:, 0])
    uv_bd = jnp.concatenate([jnp.concatenate([uvt[:, 0], z], axis=2),
                             jnp.concatenate([z, uvt[:, 1]], axis=2)], axis=1).astype(BF16)
    return dict(w_in=w_in_p, g_q=g_q[None], g_kv=g_kv[None], w_nope=w_nope, w_rope=w_rope, w_uk=uk_bd,
                w_uv=uv_bd, w_o=w_o.astype(BF16))


def kernel(x_prompt, x_sample, cache_mla_latent, cache_mla_rope, cache_win_k, cache_win_v, state_conv, page_table,
           mla_w_in, mla_g_q, mla_g_kv, mla_w_uq, mla_w_uk, mla_w_uv, mla_w_o,
           dil_w_kv, dil_w_q, dil_w_o,
           ffn_w_in, ffn_conv_w, ffn_conv_b, ffn_w_out,
           ln_mix_g, ln_mix_b, ln_ffn_g, ln_ffn_b):
    batch, seq, _ = x_prompt.shape
    nb, nt, _ = x_sample.shape
    past_len = page_table.shape[1] * PAGE_SIZE
    pos_p = jnp.arange(seq, dtype=jnp.int32)
    pos_s = jnp.repeat(past_len + jnp.arange(nt, dtype=jnp.int32), nb)
    tab_a_p = _rope_tables(pos_p, ROPE_A, LANES)
    tab_a_s = _rope_tables(pos_s, ROPE_A, LANES)
    tab_b_p = _rope_tables(pos_p, ROT_DIM_B, HEAD_DIM_B)
    tab_b_s = _rope_tables(pos_s, ROT_DIM_B, HEAD_DIM_B)

    xp = x_prompt.reshape(batch * seq, D_MODEL)
    xs = x_sample.transpose(1, 0, 2).reshape(nt * nb, D_MODEL)
    ms = nt * nb
    tm_p, tm_s = 512, 128
    cache_rope_t = cache_mla_rope.transpose(0, 1, 3, 2)

    def to_batch_major(a):
        return a.reshape(nt, nb, a.shape[-1]).transpose(1, 0, 2)

    lat_p, rope_p, lat_s, rope_s, conv_p, conv_s = [], [], [], [], [], []
    for layer in range(DEPTH):
        g_mix, b_mix = ln_mix_g[layer][None], ln_mix_b[layer][None]
        if layer < N_A_LAYERS:
            wa = _mla_weights(mla_w_in[layer], mla_g_q[layer], mla_g_kv[layer], mla_w_uq[layer], mla_w_uk[layer],
                              mla_w_uv[layer], mla_w_o[layer])
            ckv, kr, kcat, q = _mla_proj(xp, wa, tab_a_p, tm_p)
            ol = _mla_flash(q, kcat, batch, seq)
            xp = _mla_out(ol, wa, xp, g_mix, b_mix, tm_p)
            lat_p.append(ckv.reshape(batch, seq, KV_LORA))
            rope_p.append(kr[:, :ROPE_A].reshape(batch, seq, ROPE_A))

            ckv_s, kr_s, _, q_s = _mla_proj(xs, wa, tab_a_s, tm_s)
            q_b = q_s.reshape(nt, N_HEADS_A, nb, MLA_QK_W).transpose(2, 0, 1, 3).reshape(nb, nt * N_HEADS_A, MLA_QK_W)
            padrows = ((0, 0), (0, SUBLANES - nt), (0, 0))
            c_new = jnp.pad(to_batch_major(ckv_s), padrows)
            r_new = jnp.pad(to_batch_major(kr_s), padrows)
            o_b = _mla_sample_attn(page_table, q_b, c_new, r_new, cache_mla_latent, cache_rope_t, layer)
            ol_s = o_b.reshape(nb, nt, N_HEADS_A * KV_LORA).transpose(1, 0, 2).reshape(ms, N_HEADS_A * KV_LORA)
            xs = _mla_out(ol_s, wa, xs, g_mix, b_mix, tm_s)
            lat_s.append(to_batch_major(ckv_s))
            rope_s.append(to_batch_major(kr_s[:, :ROPE_A]))
        else:
            if layer == N_A_LAYERS:
                w_kv = dil_w_kv.astype(BF16)
                lens_p = tuple(min(w, seq) for w in WINDOWS)
                kv_res = _proj_rope(xp, w_kv, tab_b_p, QB_W, F32, 512, lane_blocks=True, windows=lens_p, seq=seq)
                kv_p = kv_res[0]
                new_win_k_prompt = jnp.concatenate(kv_res[1:1 + N_GROUPS], axis=-1).transpose(0, 3, 1, 2)
                new_win_v_prompt = jnp.concatenate(kv_res[1 + N_GROUPS:], axis=-1).transpose(0, 3, 1, 2)
                kv_s = _proj_rope(xs, w_kv, tab_b_s, QB_W, F32, tm_s)
                kv_s6 = kv_s.reshape(nt, nb, 2, N_GROUPS, HEADS_PER_GROUP, HEAD_DIM_B)
                lens_s = tuple(min(w, past_len) for w in WINDOWS)

                def new_rows_t(a):
                    a = a.transpose(1, 3, 2, 0, 4).reshape(nb, HEADS_PER_GROUP, N_GROUPS * nt, HEAD_DIM_B)
                    return jnp.pad(a, ((0, 0), (0, 0), (0, DS_NEW - N_GROUPS * nt), (0, 0)))

                kn_t, vn_t = new_rows_t(kv_s6[:, :, 0]), new_rows_t(kv_s6[:, :, 1])
                ck_t = cache_win_k.transpose(0, 2, 3, 1)
                cv_t = cache_win_v.transpose(0, 2, 3, 1)
            bi = layer - N_A_LAYERS
            w_q = dil_w_q[bi].astype(BF16)
            w_o = dil_w_o[bi].astype(BF16)
            q_p = _proj_rope(xp, w_q, tab_b_p, QB_W, F32, 512, lane_blocks=True)
            outs, lses = zip(*[_dil_attn_group(q_p, kv_p, batch, seq, g) for g in range(N_GROUPS)])
            xp = _dil_merge(outs, lses, w_o, xp, g_mix, b_mix, 512)

            q_s = _proj_rope(xs, w_q, tab_b_s, QB_W, BF16, tm_s)
            q_g = q_s.reshape(nt, nb, N_GROUPS, HEADS_PER_GROUP, HEAD_DIM_B).transpose(1, 3, 2, 0, 4)
            q_g = jnp.pad(q_g, ((0, 0), (0, 0), (0, DS_ROWS // SUBLANES - N_GROUPS), (0, SUBLANES - nt), (0, 0)))
            q_g = q_g.reshape(nb, HEADS_PER_GROUP, DS_ROWS, HEAD_DIM_B)
            write_cache = layer == N_A_LAYERS
            res = _dil_sample(q_g, kn_t, vn_t, ck_t, cv_t, lens_s, nt, write_cache)
            if write_cache:
                new_win_k_sample = res[1].transpose(0, 3, 1, 2)
                new_win_v_sample = res[2].transpose(0, 3, 1, 2)
            y_s = res[0][:, :, :nt].transpose(2, 0, 1, 3).reshape(ms, GROUP_W)
            xs = _proj_ln(y_s, w_o, xs, g_mix, b_mix, tm_s)

        wf = dict(w_in=ffn_w_in[layer].astype(BF16), conv_w=ffn_conv_w[layer], conv_b=ffn_conv_b[layer][None],
                  w_out=ffn_w_out[layer].astype(BF16))
        g_ffn, b_ffn = ln_ffn_g[layer][None], ln_ffn_b[layer][None]
        xp, st_p = _ffn_prompt(xp, wf, g_ffn, b_ffn, seq, 512)
        xs, st_s = _ffn_sample(xs, state_conv[layer].transpose(1, 0, 2), wf, g_ffn, b_ffn, nb, nt)
        conv_p.append(st_p)
        conv_s.append(st_s.transpose(1, 0, 2))

    y_prompt = xp.reshape(batch, seq, D_MODEL)
    y_sample = xs.reshape(nt, nb, D_MODEL).transpose(1, 0, 2)
    return (y_prompt, y_sample, jnp.stack(lat_p, 0), jnp.stack(rope_p, 0), new_win_k_prompt, new_win_v_prompt,
            jnp.stack(conv_p, 0), jnp.stack(lat_s, 0), jnp.stack(rope_s, 0), new_win_k_sample, new_win_v_sample,
            jnp.stack(conv_s, 0))
```

```python
import functools

import numpy as np
import jax
import jax.numpy as jnp
from jax import lax
from jax.experimental import pallas as pl
from jax.experimental.pallas import tpu as pltpu

F32 = jnp.float32
BF16 = jnp.bfloat16

D_MODEL = 1024
DEPTH = 4
PAGE_SIZE = 128
N_A_LAYERS = DEPTH // 2
N_HEADS_A = 16
NOPE_A = 64
ROPE_A = 32
V_DIM_A = 64
Q_LORA = 256
KV_LORA = 128
MLA_SCALE = (NOPE_A + ROPE_A) ** -0.5
WINDOWS = (128, 512, 2048)
DILATIONS = (1, 4, 16)
N_GROUPS = 3
HEADS_PER_GROUP = 8
HEAD_DIM_B = 64
N_HEADS_B = N_GROUPS * HEADS_PER_GROUP
ROT_DIM_B = HEAD_DIM_B // 4
B_SCALE = HEAD_DIM_B ** -0.5
BLOCK_B = 128
ROPE_THETA = 500000.0
D_FF = 2816
CONV_W = 3
ALPHA = (2.0 * DEPTH) ** 0.25
LN_EPS = 1e-5
RMS_EPS = 1e-6
NEG_INF = -1e30
SQRT_HALF = 0.7071067811865476
LOG2_E = 1.4426950408889634
MLA_Q_SCALE = MLA_SCALE * LOG2_E

LANES = 128
SUBLANES = 8
VMEM_LIMIT_BYTES = 48 * 1024 * 1024

GROUP_W = HEADS_PER_GROUP * HEAD_DIM_B
QB_W = N_HEADS_B * HEAD_DIM_B
MLA_QK_W = 2 * LANES
FFN_CHUNK = 256


def _params(*sem):
    return pltpu.CompilerParams(dimension_semantics=sem, vmem_limit_bytes=VMEM_LIMIT_BYTES)


def _const_spec(shape):
    nd = len(shape)
    return pl.BlockSpec(shape, lambda *_: (0,) * nd)


def _dot(a, b):
    return jnp.dot(a, b, preferred_element_type=F32)


def _dot_nt(a, b):
    return lax.dot_general(a, b, (((1,), (1,)), ((), ())), preferred_element_type=F32)


def _layer_norm(z, g, b):
    mu = jnp.mean(z, axis=-1, keepdims=True)
    zc = z - mu
    var = jnp.mean(zc * zc, axis=-1, keepdims=True)
    return zc * lax.rsqrt(var + LN_EPS) * g + b


def _rms_norm(a, g):
    return a * lax.rsqrt(jnp.mean(a * a, axis=-1, keepdims=True) + RMS_EPS) * g


def _rope_lanes(v, cos, sin_a, sin_b, half):
    return (v * cos + pltpu.roll(v, LANES - half, 1) * sin_a + pltpu.roll(v, half, 1) * sin_b)


def _rope_tables(pos, rot_dim, period):
    half = rot_dim // 2
    inv = ROPE_THETA ** (-jnp.arange(half, dtype=F32) * 2.0 / rot_dim)
    ang = pos.astype(F32)[:, None] * inv[None, :]
    cos, sin = jnp.cos(ang), jnp.sin(ang)
    t = pos.shape[0]
    ones = jnp.ones((t, period - rot_dim), F32)
    zr = lambda n: jnp.zeros((t, n), F32)
    c = jnp.concatenate([cos, cos, ones], axis=1)
    sa = jnp.concatenate([-sin, zr(period - half)], axis=1)
    sb = jnp.concatenate([zr(half), sin, zr(period - rot_dim)], axis=1)
    rep = LANES // period
    return tuple(jnp.tile(a, (1, rep)) for a in (c, sa, sb))


def _mla_proj_kernel(x_ref, win_ref, gq_ref, gkv_ref, cos_ref, sa_ref, sb_ref, wn_ref, wr_ref, wuk_ref,
                     ckv_ref, kr_ref, kcat_ref, q_ref, *, tm):
    xb = x_ref[...].astype(BF16)
    a = _dot(xb, win_ref[...])
    cq = _rms_norm(a[:, :Q_LORA], gq_ref[...])
    ckv = _rms_norm(a[:, Q_LORA:Q_LORA + KV_LORA], gkv_ref[...])
    cos, sa, sb = cos_ref[...], sa_ref[...], sb_ref[...]
    kr = _rope_lanes(a[:, Q_LORA + KV_LORA:], cos, sa, sb, ROPE_A // 2)
    ckv_ref[...] = ckv
    kr_ref[...] = kr
    kcat_ref[:, :LANES] = ckv.astype(BF16)
    kcat_ref[:, LANES:] = kr.astype(BF16)
    cqb = cq.astype(BF16)
    qn = _dot(cqb, wn_ref[...])
    qr = _dot(cqb, wr_ref[...])
    for p in range(N_HEADS_A // 2):
        ql2 = _dot(qn[:, p * LANES:(p + 1) * LANES].astype(BF16), wuk_ref[p])
        for hh in range(2):
            h = 2 * p + hh
            ql = (ql2[:, hh * LANES:(hh + 1) * LANES] * MLA_Q_SCALE).astype(BF16)
            rr = _rope_lanes(qr[:, h * LANES:(h + 1) * LANES], cos, sa, sb, ROPE_A // 2)
            rr = (rr * MLA_Q_SCALE).astype(BF16)
            for j in range(tm // LANES):
                q_ref[j, h, :, :LANES] = ql[j * LANES:(j + 1) * LANES]
                q_ref[j, h, :, LANES:] = rr[j * LANES:(j + 1) * LANES]


def _mla_proj(x, w, tables, tm):
    m = x.shape[0]
    cos, sa, sb = tables
    nper = cos.shape[0] // tm
    tab_spec = pl.BlockSpec((tm, LANES), lambda i: (i % nper, 0))
    row = lambda n: pl.BlockSpec((tm, n), lambda i: (i, 0))
    return pl.pallas_call(
        functools.partial(_mla_proj_kernel, tm=tm),
        grid=(m // tm,),
        in_specs=[row(D_MODEL), _const_spec(w["w_in"].shape), _const_spec((1, Q_LORA)), _const_spec((1, KV_LORA)),
                  tab_spec, tab_spec, tab_spec, _const_spec(w["w_nope"].shape), _const_spec(w["w_rope"].shape),
                  _const_spec(w["w_uk"].shape)],
        out_specs=[row(KV_LORA), row(LANES), row(MLA_QK_W),
                   pl.BlockSpec((tm // LANES, N_HEADS_A, LANES, MLA_QK_W), lambda i: (i, 0, 0, 0))],
        out_shape=[jax.ShapeDtypeStruct((m, KV_LORA), F32), jax.ShapeDtypeStruct((m, LANES), F32),
                   jax.ShapeDtypeStruct((m, MLA_QK_W), BF16),
                   jax.ShapeDtypeStruct((m // LANES, N_HEADS_A, LANES, MLA_QK_W), BF16)],
        compiler_params=_params("parallel"),
        name="mla_proj",
    )(x, w["w_in"], w["g_q"], w["g_kv"], cos, sa, sb, w["w_nope"], w["w_rope"], w["w_uk"])


MLA_TQ = 256
MLA_TK = 512
MLA_HEADS_PER_DOT = 4
MLA_SUB = MLA_TQ // LANES


def _mla_flash_kernel(qi_ref, kj_ref, q_ref, k_ref, o_ref, m_ref, acc_ref):
    t = pl.program_id(1)
    qi = qi_ref[t]
    kj = kj_ref[t]
    last = (qi * MLA_TQ + MLA_TQ - 1) // MLA_TK

    @pl.when(kj == 0)
    def _():
        m_ref[...] = jnp.full(m_ref.shape, NEG_INF, F32)
        acc_ref[...] = jnp.zeros(acc_ref.shape, F32)

    k = k_ref[...]
    v_ones = jnp.concatenate([k[:, :KV_LORA], jnp.ones((MLA_TK, LANES), BF16)], axis=1)

    nrow = MLA_HEADS_PER_DOT * LANES
    groups = N_HEADS_A // MLA_HEADS_PER_DOT

    def step(masked):
        for sb in range(MLA_SUB):
            if masked:
                q_pos = qi * MLA_TQ + sb * LANES + (lax.broadcasted_iota(jnp.int32, (nrow, MLA_TK), 0) & (LANES - 1))
                k_pos = kj * MLA_TK + lax.broadcasted_iota(jnp.int32, (nrow, MLA_TK), 1)
                visible = k_pos <= q_pos
            for hg in range(groups):
                rows = pl.ds((sb * groups + hg) * nrow, nrow)
                q = q_ref[sb, hg * MLA_HEADS_PER_DOT:(hg + 1) * MLA_HEADS_PER_DOT].reshape(nrow, MLA_QK_W)
                s = _dot_nt(q, k)
                if masked:
                    s = jnp.where(visible, s, NEG_INF)
                m_prev = m_ref[rows, :]
                m_new = jnp.maximum(m_prev, jnp.max(s, axis=-1, keepdims=True))
                alpha = jnp.exp2(m_prev - m_new)
                p = jnp.exp2(s - jnp.concatenate([m_new] * (MLA_TK // LANES), axis=1))
                pv = _dot(p.astype(BF16), v_ones)
                acc_ref[rows, :] = jnp.concatenate([alpha, alpha], axis=1) * acc_ref[rows, :] + pv
                m_ref[rows, :] = m_new

    @pl.when(kj != last)
    def _():
        step(False)

    @pl.when(kj == last)
    def _():
        step(True)
        for sb in range(MLA_SUB):
            for h in range(N_HEADS_A):
                a = acc_ref[pl.ds((sb * N_HEADS_A + h) * LANES, LANES), :]
                o_ref[sb * LANES:(sb + 1) * LANES, h * KV_LORA:(h + 1) * KV_LORA] = (
                    a[:, :KV_LORA] / a[:, KV_LORA:]).astype(BF16)


def _mla_flash(q, kcat, batch, seq):
    nq = seq // MLA_TQ
    nk = seq // MLA_TK
    pairs = [(i, j) for i in range(nq) for j in range((i * MLA_TQ + MLA_TQ - 1) // MLA_TK + 1)]
    qi = jnp.asarray(np.array([p[0] for p in pairs], np.int32))
    kj = jnp.asarray(np.array([p[1] for p in pairs], np.int32))
    rows = N_HEADS_A * MLA_TQ
    return pl.pallas_call(
        _mla_flash_kernel,
        grid_spec=pltpu.PrefetchScalarGridSpec(
            num_scalar_prefetch=2,
            grid=(batch, len(pairs)),
            in_specs=[pl.BlockSpec((MLA_SUB, N_HEADS_A, LANES, MLA_QK_W), lambda b, t, qi, kj: (b * nq + qi[t], 0, 0, 0)),
                      pl.BlockSpec((MLA_TK, MLA_QK_W), lambda b, t, qi, kj: (b * nk + kj[t], 0))],
            out_specs=pl.BlockSpec((MLA_TQ, N_HEADS_A * KV_LORA), lambda b, t, qi, kj: (b * nq + qi[t], 0)),
            scratch_shapes=[pltpu.VMEM((rows, LANES), F32), pltpu.VMEM((rows, KV_LORA + LANES), F32)],
        ),
        out_shape=jax.ShapeDtypeStruct((batch * seq, N_HEADS_A * KV_LORA), BF16),
        compiler_params=_params("parallel", "arbitrary"),
        name="mla_flash",
    )(qi, kj, q, kcat)


MLA_SAMPLE_SEQS = 2


def _mla_sample_kernel(pt_ref, q_ref, cn_ref, rn_ref, lat_hbm, rope_hbm, o_ref, lat_buf, rope_buf, sem,
                       *, layer, n_pages, n_new):
    step = pl.program_id(0)
    n_steps = pl.num_programs(0)
    slot = step % 2
    seqs = range(MLA_SAMPLE_SEQS)

    def lat_copy(st, sl, u, j):
        page = pt_ref[st * MLA_SAMPLE_SEQS + u, j]
        return pltpu.make_async_copy(lat_hbm.at[layer, page], lat_buf.at[sl, u, j], sem.at[sl, 0])

    def rope_copy(st, sl, u, j):
        page = pt_ref[st * MLA_SAMPLE_SEQS + u, j]
        return pltpu.make_async_copy(rope_hbm.at[layer, page],
                                     rope_buf.at[sl, u, :, pl.ds(j * PAGE_SIZE, PAGE_SIZE)], sem.at[sl, 1])

    def start_all(st, sl):
        for u in seqs:
            for j in range(n_pages):
                lat_copy(st, sl, u, j).start()
                rope_copy(st, sl, u, j).start()

    @pl.when(step == 0)
    def _():
        start_all(0, 0)

    @pl.when(step + 1 < n_steps)
    def _():
        start_all(step + 1, 1 - slot)

    for u in seqs:
        for j in range(n_pages):
            lat_copy(step, slot, u, j).wait()
            rope_copy(step, slot, u, j).wait()

    n_keys = n_pages * PAGE_SIZE
    lats, cns, scores = [], [], []
    for u in seqs:
        lat = lat_buf[slot, u].reshape(n_keys, KV_LORA).astype(BF16)
        rope_t = rope_buf[slot, u].astype(BF16)
        q = q_ref[u]
        q_lat = q[:, :KV_LORA]
        cn = cn_ref[u].astype(BF16)
        rn = rn_ref[u].astype(BF16)
        s = _dot_nt(q_lat, lat) + _dot(q[:, KV_LORA:KV_LORA + ROPE_A], rope_t)
        s_new = _dot_nt(q_lat, cn) + _dot_nt(q[:, KV_LORA:], rn)
        lats.append(lat)
        cns.append(cn)
        scores.append((s, s_new))
    probs = []
    for s, s_new in scores:
        t_q = lax.broadcasted_iota(jnp.int32, s_new.shape, 0) // N_HEADS_A
        j_k = lax.broadcasted_iota(jnp.int32, s_new.shape, 1)
        s_new = jnp.where(j_k <= t_q, s_new, NEG_INF)
        m = jnp.maximum(jnp.max(s, axis=-1, keepdims=True), jnp.max(s_new, axis=-1, keepdims=True))
        e = jnp.exp2(s - m)
        e_new = jnp.exp2(s_new - m)
        den = jnp.sum(e, axis=-1, keepdims=True) + jnp.sum(e_new, axis=-1, keepdims=True)
        probs.append(((e / den).astype(BF16), (e_new / den).astype(BF16)))
    for u in seqs:
        o_ref[u] = (_dot(probs[u][0], lats[u]) + _dot(probs[u][1], cns[u])).astype(BF16)


def _mla_sample_attn(page_table, q, c_new, r_new, cache_lat, cache_rope_t, layer):
    nb, n_pages = page_table.shape
    rows = q.shape[1]
    n_new = rows // N_HEADS_A
    ns = MLA_SAMPLE_SEQS
    assert nb % ns == 0
    return pl.pallas_call(
        functools.partial(_mla_sample_kernel, layer=layer, n_pages=n_pages, n_new=n_new),
        grid_spec=pltpu.PrefetchScalarGridSpec(
            num_scalar_prefetch=1,
            grid=(nb // ns,),
            in_specs=[pl.BlockSpec((ns, rows, MLA_QK_W), lambda b, pt: (b, 0, 0)),
                      pl.BlockSpec((ns, SUBLANES, KV_LORA), lambda b, pt: (b, 0, 0)),
                      pl.BlockSpec((ns, SUBLANES, LANES), lambda b, pt: (b, 0, 0)),
                      pl.BlockSpec(memory_space=pl.ANY),
                      pl.BlockSpec(memory_space=pl.ANY)],
            out_specs=pl.BlockSpec((ns, rows, KV_LORA), lambda b, pt: (b, 0, 0)),
            scratch_shapes=[pltpu.VMEM((2, ns, n_pages, PAGE_SIZE, KV_LORA), F32),
                            pltpu.VMEM((2, ns, ROPE_A, n_pages * PAGE_SIZE), F32),
                            pltpu.SemaphoreType.DMA((2, 2))],
        ),
        out_shape=jax.ShapeDtypeStruct((nb, rows, KV_LORA), BF16),
        compiler_params=_params("arbitrary"),
        name="mla_sample_attn",
    )(page_table, q, c_new, r_new, cache_lat, cache_rope_t)


def _mla_out_kernel(ol_ref, wuv_ref, wo_ref, x_ref, g_ref, b_ref, y_ref):
    parts = [_dot(ol_ref[:, p * 2 * KV_LORA:(p + 1) * 2 * KV_LORA], wuv_ref[p]) for p in range(N_HEADS_A // 2)]
    o = jnp.concatenate(parts, axis=1).astype(BF16)
    mix = _dot(o, wo_ref[...])
    y_ref[...] = _layer_norm(ALPHA * x_ref[...] + mix, g_ref[...], b_ref[...])


def _mla_out(ol, w, x, g, b, tm):
    m = x.shape[0]
    row = lambda n: pl.BlockSpec((tm, n), lambda i: (i, 0))
    return pl.pallas_call(
        _mla_out_kernel,
        grid=(m // tm,),
        in_specs=[row(N_HEADS_A * KV_LORA), _const_spec(w["w_uv"].shape), _const_spec(w["w_o"].shape),
                  row(D_MODEL), _const_spec((1, D_MODEL)), _const_spec((1, D_MODEL))],
        out_specs=row(D_MODEL),
        out_shape=jax.ShapeDtypeStruct((m, D_MODEL), F32),
        compiler_params=_params("parallel"),
        name="mla_out",
    )(ol, w["w_uv"], w["w_o"], x, g, b)


def _proj_ln_kernel(a_ref, wo_ref, x_ref, g_ref, b_ref, y_ref):
    mix = _dot(a_ref[...].astype(BF16), wo_ref[...])
    y_ref[...] = _layer_norm(ALPHA * x_ref[...] + mix, g_ref[...], b_ref[...])


def _proj_ln(a, wo, x, g, b, tm):
    m = x.shape[0]
    row = lambda n: pl.BlockSpec((tm, n), lambda i: (i, 0))
    return pl.pallas_call(
        _proj_ln_kernel,
        grid=(m // tm,),
        in_specs=[row(a.shape[1]), _const_spec(wo.shape), row(D_MODEL), _const_spec((1, D_MODEL)),
                  _const_spec((1, D_MODEL))],
        out_specs=row(D_MODEL),
        out_shape=jax.ShapeDtypeStruct((m, D_MODEL), F32),
        compiler_params=_params("parallel"),
        name="proj_ln",
    )(a, wo, x, g, b)


def _dil_merge_kernel(o0, o1, o2, l0, l1, l2, wo_ref, x_ref, g_ref, b_ref, y_ref):
    wide = lambda ref: jnp.concatenate([ref[j] for j in range(ref.shape[0])], axis=1)
    la, lb, lc = wide(l0), wide(l1), wide(l2)
    m = jnp.maximum(jnp.maximum(la, lb), lc)
    ea, eb, ec = jnp.exp(la - m), jnp.exp(lb - m), jnp.exp(lc - m)
    den = ea + eb + ec
    y = (ea / den) * wide(o0) + (eb / den) * wide(o1) + (ec / den) * wide(o2)
    mix = _dot(y.astype(BF16), wo_ref[...])
    y_ref[...] = _layer_norm(ALPHA * x_ref[...] + mix, g_ref[...], b_ref[...])


def _dil_merge(outs, lses, wo, x, g, b, tm):
    m = x.shape[0]
    row = lambda n: pl.BlockSpec((tm, n), lambda i: (i, 0))
    return pl.pallas_call(
        _dil_merge_kernel,
        grid=(m // tm,),
        in_specs=[pl.BlockSpec((outs[0].shape[0], tm, LANES), lambda i: (0, i, 0))] * 6
        + [_const_spec(wo.shape), row(D_MODEL), _const_spec((1, D_MODEL)), _const_spec((1, D_MODEL))],
        out_specs=row(D_MODEL),
        out_shape=jax.ShapeDtypeStruct((m, D_MODEL), F32),
        compiler_params=_params("parallel"),
        name="dil_merge",
    )(*outs, *lses, wo, x, g, b)


def _gelu_gate(acc, val):
    return 0.5 * acc * (1.0 + lax.erf(acc * SQRT_HALF)) * val


def _ffn_prompt_kernel(x_ref, xh_ref, win_ref, cw_ref, cb_ref, wout_ref, g_ref, b_ref, y_ref, st_ref,
                       *, tm, tiles_per_seq):
    i = pl.program_id(0)
    x = x_ref[...]
    xb = x.astype(BF16)
    xhb = xh_ref[...].astype(BF16)
    keep = ((i % tiles_per_seq) != 0).astype(F32)
    acc = jnp.zeros((tm, D_MODEL), F32)
    n_chunks = D_FF // FFN_CHUNK

    def up_proj(c):
        lo = c * FFN_CHUNK
        wg = win_ref[:, lo:lo + FFN_CHUNK]
        return (_dot(xb, wg), _dot(xb, win_ref[:, D_FF + lo:D_FF + lo + FFN_CHUNK]), _dot(xhb, wg))

    nxt = up_proj(0)
    for c in range(n_chunks):
        lo = c * FFN_CHUNK
        gate, val, gh = nxt
        if c + 1 < n_chunks:
            nxt = up_proj(c + 1)
        ext = jnp.concatenate([gh * keep, gate], axis=0)
        g1 = ext[SUBLANES - 1:SUBLANES - 1 + tm]
        g2 = ext[SUBLANES - 2:SUBLANES - 2 + tm]
        a = cb_ref[:, lo:lo + FFN_CHUNK] + cw_ref[0:1, lo:lo + FFN_CHUNK] * g2
        a = a + cw_ref[1:2, lo:lo + FFN_CHUNK] * g1
        a = a + cw_ref[2:3, lo:lo + FFN_CHUNK] * gate
        h = _gelu_gate(a, val).astype(BF16)
        acc = acc + _dot(h, wout_ref[lo:lo + FFN_CHUNK, :])
        st_ref[0, :, lo:lo + FFN_CHUNK] = gate[tm - SUBLANES:]
    y_ref[...] = _layer_norm(ALPHA * x + acc, g_ref[...], b_ref[...])


def _ffn_prompt(x, w, g, b, seq, tm):
    m = x.shape[0]
    tiles_per_seq = seq // tm
    hb = tm // SUBLANES
    row = lambda n: pl.BlockSpec((tm, n), lambda i: (i, 0))
    y, st = pl.pallas_call(
        functools.partial(_ffn_prompt_kernel, tm=tm, tiles_per_seq=tiles_per_seq),
        grid=(m // tm,),
        in_specs=[row(D_MODEL),
                  pl.BlockSpec((SUBLANES, D_MODEL), lambda i: (jnp.maximum(i * hb - 1, 0), 0)),
                  _const_spec(w["w_in"].shape), _const_spec((CONV_W, D_FF)), _const_spec((1, D_FF)),
                  _const_spec(w["w_out"].shape), _const_spec((1, D_MODEL)), _const_spec((1, D_MODEL))],
        out_specs=[row(D_MODEL), pl.BlockSpec((1, SUBLANES, D_FF), lambda i: (i // tiles_per_seq, 0, 0))],
        out_shape=[jax.ShapeDtypeStruct((m, D_MODEL), F32),
                   jax.ShapeDtypeStruct((m // seq, SUBLANES, D_FF), F32)],
        compiler_params=_params("arbitrary"),
        name="ffn_prompt",
    )(x, x, w["w_in"], w["conv_w"], w["conv_b"], w["w_out"], g, b)
    return y, st[:, SUBLANES - (CONV_W - 1):]


def _ffn_sample_kernel(x_ref, s_ref, win_ref, cw_ref, cb_ref, wout_ref, g_ref, b_ref, y_ref, st_ref, *, nb, nt):
    x = x_ref[...]
    xb = x.astype(BF16)
    acc = jnp.zeros((nt * nb, D_MODEL), F32)
    for c in range(D_FF // FFN_CHUNK):
        lo = c * FFN_CHUNK
        gate = _dot(xb, win_ref[:, lo:lo + FFN_CHUNK])
        val = _dot(xb, win_ref[:, D_FF + lo:D_FF + lo + FFN_CHUNK])
        s0 = s_ref[0, :, lo:lo + FFN_CHUNK]
        s1 = s_ref[1, :, lo:lo + FFN_CHUNK]
        g1 = jnp.concatenate([s1, gate[:(nt - 1) * nb]], axis=0)
        g2 = jnp.concatenate([s0, s1, gate[:(nt - 2) * nb]], axis=0)
        a = cb_ref[:, lo:lo + FFN_CHUNK] + cw_ref[0:1, lo:lo + FFN_CHUNK] * g2
        a = a + cw_ref[1:2, lo:lo + FFN_CHUNK] * g1
        a = a + cw_ref[2:3, lo:lo + FFN_CHUNK] * gate
        h = _gelu_gate(a, val).astype(BF16)
        acc = acc + _dot(h, wout_ref[lo:lo + FFN_CHUNK, :])
        st_ref[0, :, lo:lo + FFN_CHUNK] = gate[(nt - 2) * nb:(nt - 1) * nb]
        st_ref[1, :, lo:lo + FFN_CHUNK] = gate[(nt - 1) * nb:]
    y_ref[...] = _layer_norm(ALPHA * x + acc, g_ref[...], b_ref[...])


def _ffn_sample(x, state, w, g, b, nb, nt):
    m = x.shape[0]
    return pl.pallas_call(
        functools.partial(_ffn_sample_kernel, nb=nb, nt=nt),
        out_shape=[jax.ShapeDtypeStruct((m, D_MODEL), F32), jax.ShapeDtypeStruct((CONV_W - 1, nb, D_FF), F32)],
        compiler_params=pltpu.CompilerParams(vmem_limit_bytes=VMEM_LIMIT_BYTES),
        name="ffn_sample",
    )(x, state, w["w_in"], w["conv_w"], w["conv_b"], w["w_out"], g, b)


PROJ_CHUNK = 512


def _proj_rope_kernel(x_ref, w_ref, cos_ref, sa_ref, sb_ref, o_ref, *win_refs, n, rope_cols, lane_blocks, tm,
                      windows, tiles_per_seq):
    xb = x_ref[...].astype(BF16)
    cos, sa, sb = cos_ref[...], sa_ref[...], sb_ref[...]
    is_last_tile = (pl.program_id(0) % tiles_per_seq) == tiles_per_seq - 1 if windows else None
    for c in range(n // PROJ_CHUNK):
        lo = c * PROJ_CHUNK
        y = _dot(xb, w_ref[:, lo:lo + PROJ_CHUNK])
        for k in range(PROJ_CHUNK // LANES):
            slab = lo // LANES + k
            yk = y[:, k * LANES:(k + 1) * LANES]
            if lo < rope_cols:
                yk = _rope_lanes(yk, cos, sa, sb, ROT_DIM_B // 2)
            if lane_blocks:
                o_ref[slab] = yk.astype(o_ref.dtype)
            else:
                o_ref[:, lo + k * LANES:lo + (k + 1) * LANES] = yk.astype(o_ref.dtype)
            if windows:
                kv_sel, rest = divmod(slab, N_GROUPS * GROUP_SLABS)
                g, j = divmod(rest, GROUP_SLABS)
                dst = win_refs[kv_sel * N_GROUPS + g]
                width = min(windows[g], tm)

                def emit(dst=dst, j=j, width=width, yk=yk):
                    t = jnp.transpose(yk[tm - width:])
                    dst[0, 2 * j:2 * j + 2] = t.reshape(2, HEAD_DIM_B, width)

                if windows[g] >= tiles_per_seq * tm:
                    emit()
                else:
                    pl.when(is_last_tile)(emit)


def _proj_rope(x, w, tables, rope_cols, out_dtype, tm, lane_blocks=False, windows=None, seq=None):
    m = x.shape[0]
    n = w.shape[1]
    cos, sa, sb = tables
    nper = cos.shape[0] // tm
    tab_spec = pl.BlockSpec((tm, LANES), lambda i: (i % nper, 0))
    if lane_blocks:
        out_specs = [pl.BlockSpec((n // LANES, tm, LANES), lambda i: (0, i, 0))]
        out_shape = [jax.ShapeDtypeStruct((n // LANES, m, LANES), out_dtype)]
    else:
        out_specs = [pl.BlockSpec((tm, n), lambda i: (i, 0))]
        out_shape = [jax.ShapeDtypeStruct((m, n), out_dtype)]
    tiles_per_seq = None
    if windows:
        tiles_per_seq = seq // tm
        assert all(wl == seq or wl <= tm for wl in windows) and n == 2 * QB_W
        for _ in range(2):
            for wl in windows:
                if wl == seq:
                    spec = pl.BlockSpec((1, HEADS_PER_GROUP, HEAD_DIM_B, tm),
                                        lambda i: (i // tiles_per_seq, 0, 0, i % tiles_per_seq))
                else:
                    spec = pl.BlockSpec((1, HEADS_PER_GROUP, HEAD_DIM_B, wl), lambda i: (i // tiles_per_seq, 0, 0, 0))
                out_specs.append(spec)
                out_shape.append(jax.ShapeDtypeStruct((m // seq, HEADS_PER_GROUP, HEAD_DIM_B, wl), F32))
    res = pl.pallas_call(
        functools.partial(_proj_rope_kernel, n=n, rope_cols=rope_cols, lane_blocks=lane_blocks, tm=tm,
                          windows=windows, tiles_per_seq=tiles_per_seq),
        grid=(m // tm,),
        in_specs=[pl.BlockSpec((tm, D_MODEL), lambda i: (i, 0)), _const_spec(w.shape), tab_spec, tab_spec, tab_spec],
        out_specs=out_specs,
        out_shape=out_shape,
        compiler_params=_params("arbitrary" if windows else "parallel"),
        name="proj_rope",
    )(x, w, cos, sa, sb)
    return res if windows else res[0]


GROUP_SLABS = GROUP_W // LANES


def _dil_attn_kernel(q_ref, k_ref, v_ref, o_ref, l_ref, *, d, span):
    seq = q_ref.shape[2]
    nblk = seq // (BLOCK_B * d)
    has_prev = nblk > 1
    nk = 2 * BLOCK_B if has_prev else BLOCK_B
    a = lax.broadcasted_iota(jnp.int32, (BLOCK_B, nk), 0)
    c = lax.broadcasted_iota(jnp.int32, (BLOCK_B, nk), 1)
    diff = a + (nk - BLOCK_B) - c
    in_band = (diff >= 0) & (diff <= span)

    r_low_count = min(d, SUBLANES)

    def phase_rows(ref, n, r):
        r_hi, r_lo = r
        start = pl.multiple_of(n * (BLOCK_B * d) + r_hi * SUBLANES, SUBLANES) + r_lo
        return jnp.concatenate([ref[j, 0, pl.ds(start, BLOCK_B, stride=d), :] for j in range(GROUP_SLABS)], axis=1)

    def body(it, carry, r_lo):
        r = (it // nblk, r_lo)
        n = it % nblk
        q = phase_rows(q_ref, n, r).astype(BF16)
        if has_prev:
            n_prev = jnp.maximum(n - 1, 0)
            k = jnp.concatenate([phase_rows(k_ref, n_prev, r), phase_rows(k_ref, n, r)], axis=0).astype(BF16)
            v = jnp.concatenate([phase_rows(v_ref, n_prev, r), phase_rows(v_ref, n, r)], axis=0).astype(BF16)
            mask = in_band & ((n - 1) * BLOCK_B + c >= 0)
        else:
            k = phase_rows(k_ref, n, r).astype(BF16)
            v = phase_rows(v_ref, n, r).astype(BF16)
            mask = in_band
        head = lambda a, h: a[:, h * HEAD_DIM_B:(h + 1) * HEAD_DIM_B]
        scores = [_dot_nt(head(q, h), head(k, h)) for h in range(HEADS_PER_GROUP)]
        probs, lses = [], []
        for s in scores:
            s = jnp.where(mask, s * B_SCALE, NEG_INF)
            m = jnp.max(s, axis=-1, keepdims=True)
            e = jnp.exp(s - m)
            den = jnp.sum(e, axis=-1, keepdims=True)
            probs.append((e / den).astype(BF16))
            lses.append(jnp.broadcast_to(m + jnp.log(den), (BLOCK_B, HEAD_DIM_B)))
        outs = [_dot(probs[h], head(v, h)) for h in range(HEADS_PER_GROUP)]
        start = pl.multiple_of(n * (BLOCK_B * d) + r[0] * SUBLANES, SUBLANES) + r_lo
        rows = pl.ds(start, BLOCK_B, stride=d)
        for j in range(GROUP_SLABS):
            o_ref[j, 0, rows, :] = jnp.concatenate(outs[2 * j:2 * j + 2], axis=1)
            l_ref[j, 0, rows, :] = jnp.concatenate(lses[2 * j:2 * j + 2], axis=1)
        return carry

    for r_lo in range(r_low_count):
        trips = (d // r_low_count) * nblk
        lax.fori_loop(0, trips, functools.partial(body, r_lo=r_lo), 0, unroll=2 if trips > 2 else 1)


def _dil_attn_group(q, kv, batch, seq, g):
    d = DILATIONS[g]
    span = WINDOWS[g] // d
    nq = QB_W // GROUP_W
    blk = (GROUP_SLABS, 1, seq, LANES)
    col = lambda j: pl.BlockSpec(blk, lambda b: (j, b, 0, 0))
    shape = jax.ShapeDtypeStruct((GROUP_SLABS, batch, seq, LANES), F32)
    q4 = q.reshape(q.shape[0], batch, seq, LANES)
    kv4 = kv.reshape(kv.shape[0], batch, seq, LANES)
    o, l = pl.pallas_call(
        functools.partial(_dil_attn_kernel, d=d, span=span),
        grid=(batch,),
        in_specs=[col(g), col(g), col(nq + g)],
        out_specs=[col(0), col(0)],
        out_shape=[shape, shape],
        compiler_params=_params("parallel"),
        name=f"dil_attn_g{g}",
    )(q4, kv4, kv4)
    return o.reshape(GROUP_SLABS, batch * seq, LANES), l.reshape(GROUP_SLABS, batch * seq, LANES)


DS_HEADS = 4
DS_ROWS = 4 * SUBLANES
DS_NEW = 2 * SUBLANES


def _dil_sample_bias(lens, n_new):
    bias = np.full((DS_ROWS, sum(lens)), NEG_INF, np.float32)
    bias_new = np.full((DS_ROWS, DS_NEW), NEG_INF, np.float32)
    off = 0
    for g, ln in enumerate(lens):
        d = DILATIONS[g]
        span = WINDOWS[g] // d
        for t in range(n_new):
            rel = ln + t - np.arange(ln)
            ok = (rel % d == 0) & (rel // d <= span)
            bias[g * SUBLANES + t, off:off + ln] = np.where(ok, 0.0, NEG_INF)
            for t2 in range(t + 1):
                if (t - t2) % d == 0 and (t - t2) // d <= span:
                    bias_new[g * SUBLANES + t, n_new * g + t2] = 0.0
        off += ln
    return bias, bias_new


def _shift_append(src, new_rows, dst_ref, hh, lens, n_new):
    hd = src.shape[0]
    pad = jnp.concatenate([new_rows, jnp.zeros((new_rows.shape[0], LANES - hd), F32)], axis=1)
    pad = jnp.concatenate([pad, jnp.zeros((LANES - new_rows.shape[0], LANES), F32)], axis=0)
    new_t = jnp.transpose(pad)[:hd]
    nblk = src.shape[1] // LANES
    ends = {}
    off = 0
    for g, ln in enumerate(lens):
        off += ln
        ends[off // LANES - 1] = g
    lane = lax.broadcasted_iota(jnp.int32, (src.shape[0], LANES), 1)
    rolled = [pltpu.roll(src[:, cb * LANES:(cb + 1) * LANES], LANES - n_new, 1) for cb in range(nblk)]
    for cb in range(nblk):
        if cb in ends:
            tail = pltpu.roll(new_t, LANES - n_new - n_new * ends[cb], 1)
        else:
            tail = rolled[cb + 1]
        dst_ref[0, hh, :, cb * LANES:(cb + 1) * LANES] = jnp.where(lane < LANES - n_new, rolled[cb], tail)


def _dil_sample_kernel(q_ref, kn_ref, vn_ref, bias_ref, bn_ref, k_ref, v_ref, *out_refs, lens, n_new, write_cache):
    y_ref = out_refs[0]
    bias = bias_ref[...]
    bias_new = bn_ref[...]
    heads = range(q_ref.shape[1])
    scores = [(_dot(q_ref[0, hh], k_ref[0, hh].astype(BF16)), _dot_nt(q_ref[0, hh], kn_ref[0, hh].astype(BF16)))
              for hh in heads]
    probs, lses = [], []
    for s, s_new in scores:
        s = s * B_SCALE + bias
        s_new = s_new * B_SCALE + bias_new
        m = jnp.maximum(jnp.max(s, axis=-1, keepdims=True), jnp.max(s_new, axis=-1, keepdims=True))
        e = jnp.exp(s - m)
        e_new = jnp.exp(s_new - m)
        den = jnp.sum(e, axis=-1, keepdims=True) + jnp.sum(e_new, axis=-1, keepdims=True)
        probs.append(((e / den).astype(BF16), (e_new / den).astype(BF16)))
        lses.append(m + jnp.log(den))
    outs = [_dot_nt(probs[hh][0], v_ref[0, hh].astype(BF16)) + _dot(probs[hh][1], vn_ref[0, hh].astype(BF16))
            for hh in heads]
    grp = lambda a, g: a[g * SUBLANES:(g + 1) * SUBLANES]
    for hh in heads:
        lse, o = lses[hh], outs[hh]
        mm = jnp.maximum(jnp.maximum(grp(lse, 0), grp(lse, 1)), grp(lse, 2))
        ws = [jnp.exp(grp(lse, g) - mm) for g in range(N_GROUPS)]
        wsum = ws[0] + ws[1] + ws[2]
        y_ref[0, hh] = sum((ws[g] / wsum) * grp(o, g) for g in range(N_GROUPS))
    if write_cache:
        for hh in heads:
            _shift_append(k_ref[0, hh], kn_ref[0, hh], out_refs[1], hh, lens, n_new)
            _shift_append(v_ref[0, hh], vn_ref[0, hh], out_refs[2], hh, lens, n_new)


def _dil_sample(q, kn_t, vn_t, ck_t, cv_t, lens, n_new, write_cache):
    nb, hp, hd, rows = ck_t.shape
    assert all(ln % LANES == 0 for ln in lens) and sum(lens) == rows and N_GROUPS * n_new <= DS_NEW
    bias, bias_new = _dil_sample_bias(lens, n_new)
    heads = DS_HEADS if write_cache else hp
    cache_spec = pl.BlockSpec((1, heads, hd, rows), lambda b, j: (b, j, 0, 0))
    new_spec = pl.BlockSpec((1, heads, DS_NEW, hd), lambda b, j: (b, j, 0, 0))
    q_spec = pl.BlockSpec((1, heads, DS_ROWS, hd), lambda b, j: (b, j, 0, 0))
    y_spec = pl.BlockSpec((1, heads, SUBLANES, hd), lambda b, j: (b, j, 0, 0))
    out_specs = [y_spec]
    out_shape = [jax.ShapeDtypeStruct((nb, hp, SUBLANES, hd), F32)]
    if write_cache:
        out_specs += [cache_spec, cache_spec]
        out_shape += [jax.ShapeDtypeStruct(ck_t.shape, F32)] * 2
    return pl.pallas_call(
        functools.partial(_dil_sample_kernel, lens=lens, n_new=n_new, write_cache=write_cache),
        grid=(nb, hp // heads),
        in_specs=[q_spec, new_spec, new_spec, _const_spec(bias.shape), _const_spec(bias_new.shape),
                  cache_spec, cache_spec],
        out_specs=out_specs,
        out_shape=out_shape,
        compiler_params=_params("parallel", "parallel"),
        name="dil_sample_update" if write_cache else "dil_sample",
    )(q, kn_t, vn_t, jnp.asarray(bias), jnp.asarray(bias_new), ck_t, cv_t)


def _mla_weights(w_in, g_q, g_kv, w_uq, w_uk, w_uv, w_o):
    w_in_p = jnp.pad(w_in, ((0, 0), (0, 4 * LANES - w_in.shape[1]))).astype(BF16)
    uq = w_uq.reshape(Q_LORA, N_HEADS_A, NOPE_A + ROPE_A)
    w_nope = uq[:, :, :NOPE_A].reshape(Q_LORA, N_HEADS_A * NOPE_A).astype(BF16)
    w_rope = jnp.pad(uq[:, :, NOPE_A:], ((0, 0), (0, 0), (0, LANES - ROPE_A)))
    w_rope = w_rope.reshape(Q_LORA, N_HEADS_A * LANES).astype(BF16)
    ukt = w_uk.transpose(1, 2, 0).reshape(N_HEADS_A // 2, 2, NOPE_A, KV_LORA)
    z = jnp.zeros_like(ukt[:, 0])
    uk_bd = jnp.concatenate([jnp.concatenate([ukt[:, 0], z], axis=2),
                             jnp.concatenate([z, ukt[:, 1]], axis=2)], axis=1).astype(BF16)
    uvt = w_uv.transpose(1, 0, 2).reshape(N_HEADS_A // 2, 2, KV_LORA, V_DIM_A)
    z = jnp.zeros_like(uvt[:, 0])
    uv_bd = jnp.concatenate([jnp.concatenate([uvt[:, 0], z], axis=2),
                             jnp.concatenate([z, uvt[:, 1]], axis=2)], axis=1).astype(BF16)
    return dict(w_in=w_in_p, g_q=g_q[None], g_kv=g_kv[None], w_nope=w_nope, w_rope=w_rope, w_uk=uk_bd,
                w_uv=uv_bd, w_o=w_o.astype(BF16))


def kernel(x_prompt, x_sample, cache_mla_latent, cache_mla_rope, cache_win_k, cache_win_v, state_conv, page_table,
           mla_w_in, mla_g_q, mla_g_kv, mla_w_uq, mla_w_uk, mla_w_uv, mla_w_o,
           dil_w_kv, dil_w_q, dil_w_o,
           ffn_w_in, ffn_conv_w, ffn_conv_b, ffn_w_out,
           ln_mix_g, ln_mix_b, ln_ffn_g, ln_ffn_b):
    batch, seq, _ = x_prompt.shape
    nb, nt, _ = x_sample.shape
    past_len = page_table.shape[1] * PAGE_SIZE
    pos_p = jnp.arange(seq, dtype=jnp.int32)
    pos_s = jnp.repeat(past_len + jnp.arange(nt, dtype=jnp.int32), nb)
    tab_a_p = _rope_tables(pos_p, ROPE_A, LANES)
    tab_a_s = _rope_tables(pos_s, ROPE_A, LANES)
    tab_b_p = _rope_tables(pos_p, ROT_DIM_B, HEAD_DIM_B)
    tab_b_s = _rope_tables(pos_s, ROT_DIM_B, HEAD_DIM_B)

    xp = x_prompt.reshape(batch * seq, D_MODEL)
    xs = x_sample.transpose(1, 0, 2).reshape(nt * nb, D_MODEL)
    ms = nt * nb
    tm_p, tm_s = 512, 128
    cache_rope_t = cache_mla_rope.transpose(0, 1, 3, 2)

    def to_batch_major(a):
        return a.reshape(nt, nb, a.shape[-1]).transpose(1, 0, 2)

    lat_p, rope_p, lat_s, rope_s, conv_p, conv_s = [], [], [], [], [], []
    for layer in range(DEPTH):
        g_mix, b_mix = ln_mix_g[layer][None], ln_mix_b[layer][None]
        if layer < N_A_LAYERS:
            wa = _mla_weights(mla_w_in[layer], mla_g_q[layer], mla_g_kv[layer], mla_w_uq[layer], mla_w_uk[layer],
                              mla_w_uv[layer], mla_w_o[layer])
            ckv, kr, kcat, q = _mla_proj(xp, wa, tab_a_p, tm_p)
            ol = _mla_flash(q, kcat, batch, seq)
            xp = _mla_out(ol, wa, xp, g_mix, b_mix, tm_p)
            lat_p.append(ckv.reshape(batch, seq, KV_LORA))
            rope_p.append(kr[:, :ROPE_A].reshape(batch, seq, ROPE_A))

            ckv_s, kr_s, _, q_s = _mla_proj(xs, wa, tab_a_s, tm_s)
            q_b = q_s.reshape(nt, N_HEADS_A, nb, MLA_QK_W).transpose(2, 0, 1, 3).reshape(nb, nt * N_HEADS_A, MLA_QK_W)
            padrows = ((0, 0), (0, SUBLANES - nt), (0, 0))
            c_new = jnp.pad(to_batch_major(ckv_s), padrows)
            r_new = jnp.pad(to_batch_major(kr_s), padrows)
            o_b = _mla_sample_attn(page_table, q_b, c_new, r_new, cache_mla_latent, cache_rope_t, layer)
            ol_s = o_b.reshape(nb, nt, N_HEADS_A * KV_LORA).transpose(1, 0, 2).reshape(ms, N_HEADS_A * KV_LORA)
            xs = _mla_out(ol_s, wa, xs, g_mix, b_mix, tm_s)
            lat_s.append(to_batch_major(ckv_s))
            rope_s.append(to_batch_major(kr_s[:, :ROPE_A]))
        else:
            if layer == N_A_LAYERS:
                w_kv = dil_w_kv.astype(BF16)
                lens_p = tuple(min(w, seq) for w in WINDOWS)
                kv_res = _proj_rope(xp, w_kv, tab_b_p, QB_W, F32, 512, lane_blocks=True, windows=lens_p, seq=seq)
                kv_p = kv_res[0]
                new_win_k_prompt = jnp.concatenate(kv_res[1:1 + N_GROUPS], axis=-1).transpose(0, 3, 1, 2)
                new_win_v_prompt = jnp.concatenate(kv_res[1 + N_GROUPS:], axis=-1).transpose(0, 3, 1, 2)
                kv_s = _proj_rope(xs, w_kv, tab_b_s, QB_W, F32, tm_s)
                kv_s6 = kv_s.reshape(nt, nb, 2, N_GROUPS, HEADS_PER_GROUP, HEAD_DIM_B)
                lens_s = tuple(min(w, past_len) for w in WINDOWS)

                def new_rows_t(a):
                    a = a.transpose(1, 3, 2, 0, 4).reshape(nb, HEADS_PER_GROUP, N_GROUPS * nt, HEAD_DIM_B)
                    return jnp.pad(a, ((0, 0), (0, 0), (0, DS_NEW - N_GROUPS * nt), (0, 0)))

                kn_t, vn_t = new_rows_t(kv_s6[:, :, 0]), new_rows_t(kv_s6[:, :, 1])
                ck_t = cache_win_k.transpose(0, 2, 3, 1)
                cv_t = cache_win_v.transpose(0, 2, 3, 1)
            bi = layer - N_A_LAYERS
            w_q = dil_w_q[bi].astype(BF16)
            w_o = dil_w_o[bi].astype(BF16)
            q_p = _proj_rope(xp, w_q, tab_b_p, QB_W, F32, 512, lane_blocks=True)
            outs, lses = zip(*[_dil_attn_group(q_p, kv_p, batch, seq, g) for g in range(N_GROUPS)])
            xp = _dil_merge(outs, lses, w_o, xp, g_mix, b_mix, 512)

            q_s = _proj_rope(xs, w_q, tab_b_s, QB_W, BF16, tm_s)
            q_g = q_s.reshape(nt, nb, N_GROUPS, HEADS_PER_GROUP, HEAD_DIM_B).transpose(1, 3, 2, 0, 4)
            q_g = jnp.pad(q_g, ((0, 0), (0, 0), (0, DS_ROWS // SUBLANES - N_GROUPS), (0, SUBLANES - nt), (0, 0)))
            q_g = q_g.reshape(nb, HEADS_PER_GROUP, DS_ROWS, HEAD_DIM_B)
            write_cache = layer == N_A_LAYERS
            res = _dil_sample(q_g, kn_t, vn_t, ck_t, cv_t, lens_s, nt, write_cache)
            if write_cache:
                new_win_k_sample = res[1].transpose(0, 3, 1, 2)
                new_win_v_sample = res[2].transpose(0, 3, 1, 2)
            y_s = res[0][:, :, :nt].transpose(2, 0, 1, 3).reshape(ms, GROUP_W)
            xs = _proj_ln(y_s, w_o, xs, g_mix, b_mix, tm_s)

        wf = dict(w_in=ffn_w_in[layer].astype(BF16), conv_w=ffn_conv_w[layer], conv_b=ffn_conv_b[layer][None],
                  w_out=ffn_w_out[layer].astype(BF16))
        g_ffn, b_ffn = ln_ffn_g[layer][None], ln_ffn_b[layer][None]
        xp, st_p = _ffn_prompt(xp, wf, g_ffn, b_ffn, seq, 512)
        xs, st_s = _ffn_sample(xs, state_conv[layer].transpose(1, 0, 2), wf, g_ffn, b_ffn, nb, nt)
        conv_p.append(st_p)
        conv_s.append(st_s.transpose(1, 0, 2))

    y_prompt = xp.reshape(batch, seq, D_MODEL)
    y_sample = xs.reshape(nt, nb, D_MODEL).transpose(1, 0, 2)
    return (y_prompt, y_sample, jnp.stack(lat_p, 0), jnp.stack(rope_p, 0), new_win_k_prompt, new_win_v_prompt,
            jnp.stack(conv_p, 0), jnp.stack(lat_s, 0), jnp.stack(rope_s, 0), new_win_k_sample, new_win_v_sample,
            jnp.stack(conv_s, 0))
```
